```python
import math
import jax, jax.numpy as jnp
from jax import lax
import numpy as np

D_MODEL = 1024
BATCH = 4
SEQ = 4096
DEPTH = 4
DEC_BATCH = 128
DEC_SEQ = 8
PAST_LEN = 2048
PAGE_SIZE = 128

N_MIXERS = 2
N_FOX = (DEPTH + N_MIXERS - 1) // N_MIXERS
N_SSD = DEPTH // N_MIXERS
FOX_HEAD_DIM = 64
FOX_HEADS = D_MODEL // FOX_HEAD_DIM
FOX_WIDTH = FOX_HEADS * FOX_HEAD_DIM
FOX_SCALE = FOX_HEAD_DIM ** -0.5
Q_BLOCK = 128
SSD_EXPAND = 2
D_INNER = SSD_EXPAND * D_MODEL
SSD_HEAD_DIM = 64
SSD_HEADS = D_INNER // SSD_HEAD_DIM
SSD_GROUPS = 8
SSD_HEADS_PER_GROUP = SSD_HEADS // SSD_GROUPS
D_STATE = 128
CONV_K = 4
CONV_DIM = D_INNER + 2 * SSD_GROUPS * D_STATE
SSD_IN_DIM = D_INNER + CONV_DIM + SSD_HEADS
SSD_CHUNK = 128
RMS_EPS = 1e-5
N_EXPERTS = 32
TOP_K = 4
D_FF = D_MODEL
SWIGLU_LIMIT = 7.0
GLU_ALPHA = 1.702
MOE_BLOCK = 128
PLE_DIM = 256
LN_EPS = 1e-5
DEEPNORM_ALPHA = (2 * DEPTH) ** 0.25
DEEPNORM_BETA = (8 * DEPTH) ** -0.25

kernel_name = 'fox_ssd_moe_hybrid_step'

F32 = jnp.float32


def layer_norm(x, g, b):
    xf = x.astype(F32)
    mu = jnp.mean(xf, axis=-1, keepdims=True)
    var = jnp.mean(jnp.square(xf - mu), axis=-1, keepdims=True)
    return ((xf - mu) * lax.rsqrt(var + LN_EPS) * g + b).astype(x.dtype)


def fox_project(x, w_in, b_f):
    b, L = x.shape[:2]
    h = x @ w_in
    q = h[..., :FOX_WIDTH].reshape(b, L, FOX_HEADS, FOX_HEAD_DIM)
    k = h[..., FOX_WIDTH:2 * FOX_WIDTH].reshape(b, L, FOX_HEADS, FOX_HEAD_DIM)
    v = h[..., 2 * FOX_WIDTH:3 * FOX_WIDTH].reshape(b, L, FOX_HEADS, FOX_HEAD_DIM)
    logf = jax.nn.log_sigmoid(h[..., 3 * FOX_WIDTH:].astype(F32) + b_f.astype(F32))
    return q, k, v, logf


def fox_attend(q, k, v, c_q, c_k, pos_q, pos_k):
    s = jnp.einsum('bqhd,bkhd->bhqk', q, k).astype(F32) * FOX_SCALE
    s = s + jnp.swapaxes(c_q, 1, 2)[..., :, None] - jnp.swapaxes(c_k, 1, 2)[..., None, :]
    s = jnp.where(pos_q[:, None] >= pos_k[None, :], s, -jnp.inf)
    p = jax.nn.softmax(s, axis=-1)
    return jnp.einsum('bhqk,bkhd->bqhd', p.astype(v.dtype), v)


def fox_prompt(x, w_in, b_f, w_out):
    q, k, v, logf = fox_project(x, w_in, b_f)
    b, L = x.shape[:2]
    c = jnp.cumsum(logf, axis=1)
    nb = L // Q_BLOCK
    qb = jnp.swapaxes(q.reshape(b, nb, Q_BLOCK, FOX_HEADS, FOX_HEAD_DIM), 0, 1)
    cb = jnp.swapaxes(c.reshape(b, nb, Q_BLOCK, FOX_HEADS), 0, 1)
    pos_k = jnp.arange(L)

    def block(args):
        qi, ci, i = args
        return fox_attend(qi, k, v, ci, c, i * Q_BLOCK + jnp.arange(Q_BLOCK), pos_k)

    o = lax.map(block, (qb, cb, jnp.arange(nb)))
    o = jnp.swapaxes(o, 0, 1).reshape(b, L, FOX_WIDTH)
    return o @ w_out, k, v, logf.astype(x.dtype)


def fox_sample(x, ck, cv, clf, page_table, w_in, b_f, w_out):
    q, k, v, logf = fox_project(x, w_in, b_f)
    bd, L = x.shape[:2]
    past = page_table.shape[1] * PAGE_SIZE
    k_all = jnp.concatenate([ck[page_table].reshape(bd, past, FOX_HEADS, FOX_HEAD_DIM).astype(k.dtype), k], axis=1)
    v_all = jnp.concatenate([cv[page_table].reshape(bd, past, FOX_HEADS, FOX_HEAD_DIM).astype(v.dtype), v], axis=1)
    lf_all = jnp.concatenate([clf[page_table].reshape(bd, past, FOX_HEADS).astype(F32), logf], axis=1)
    c_all = jnp.cumsum(lf_all, axis=1)
    o = fox_attend(q, k_all, v_all, c_all[:, past:], c_all, past + jnp.arange(L), jnp.arange(past + L))
    o = o.reshape(bd, L, FOX_WIDTH)
    return o @ w_out, k, v, logf.astype(clf.dtype)


def ssd_chunked(xs, dt, a_log, Bm, Cm, init_state, chunk):
    b, L = xs.shape[:2]
    nc = L // chunk
    G, R, P, N = SSD_GROUPS, SSD_HEADS_PER_GROUP, SSD_HEAD_DIM, D_STATE
    a = dt * (-jnp.exp(a_log.astype(F32)))
    xr = xs.reshape(b, nc, chunk, G, R, P)
    dtr = dt.reshape(b, nc, chunk, G, R)
    acum = jnp.cumsum(a.reshape(b, nc, chunk, G, R), axis=2)
    Br = Bm.reshape(b, nc, chunk, G, N)
    Cr = Cm.reshape(b, nc, chunk, G, N)
    seg = acum[:, :, :, None] - acum[:, :, None]
    causal = jnp.tril(jnp.ones((chunk, chunk), bool))[:, :, None, None]
    decay = jnp.exp(jnp.where(causal, seg, -jnp.inf))
    cb = jnp.einsum('bclgn,bcsgn->bclsg', Cr, Br).astype(F32)
    wmat = cb[..., None] * decay * dtr[:, :, None]
    y_diag = jnp.einsum('bclsgr,bcsgrp->bclgrp', wmat, xr)
    decay_end = jnp.exp(acum[:, :, -1:] - acum) * dtr
    states = jnp.einsum('bclgn,bclgr,bclgrp->bcgrpn', Br, decay_end, xr).astype(F32)
    chunk_decay = jnp.exp(acum[:, :, -1])

    def step(s, inp):
        st, dc = inp
        return dc[..., None, None] * s + st, s

    init = init_state.astype(F32).reshape(b, G, R, P, N)
    final, starts = lax.scan(step, init, (jnp.moveaxis(states, 1, 0), jnp.moveaxis(chunk_decay, 1, 0)))
    starts = jnp.moveaxis(starts, 0, 1)
    y_off = jnp.einsum('bclgn,bcgrpn->bclgrp', Cr, starts) * jnp.exp(acum)[..., None]
    y = (y_diag + y_off).reshape(b, L, SSD_HEADS, P)
    return y, final.reshape(b, SSD_HEADS, P, N).astype(init_state.dtype)


def ssd_mixer(x, ssm0, conv0, w_in, conv_w, conv_b, dt_bias, a_log, d_skip, norm_w, w_out):
    b, L = x.shape[:2]
    GN = SSD_GROUPS * D_STATE
    h = x @ w_in
    z = h[..., :D_INNER]
    xbc = h[..., D_INNER:D_INNER + CONV_DIM]
    dt = jax.nn.softplus(h[..., D_INNER + CONV_DIM:].astype(F32) + dt_bias.astype(F32))
    xp = jnp.concatenate([conv0.astype(xbc.dtype), xbc], axis=1)
    conv = conv_b
    for t in range(CONV_K):
        conv = conv + xp[:, t:t + L] * conv_w[t]
    xbc = jax.nn.silu(conv)
    xs = xbc[..., :D_INNER].reshape(b, L, SSD_HEADS, SSD_HEAD_DIM)
    Bm = xbc[..., D_INNER:D_INNER + GN].reshape(b, L, SSD_GROUPS, D_STATE)
    Cm = xbc[..., D_INNER + GN:].reshape(b, L, SSD_GROUPS, D_STATE)
    chunk = SSD_CHUNK if L % SSD_CHUNK == 0 else L
    y, ssm1 = ssd_chunked(xs, dt, a_log, Bm, Cm, ssm0, chunk)
    y = y + d_skip[:, None] * xs
    y = y.reshape(b, L, D_INNER) * jax.nn.silu(z)
    yg = y.astype(F32).reshape(b, L, SSD_GROUPS, D_INNER // SSD_GROUPS)
    yg = yg * lax.rsqrt(jnp.mean(jnp.square(yg), axis=-1, keepdims=True) + RMS_EPS)
    y = (yg.reshape(b, L, D_INNER) * norm_w).astype(x.dtype)
    return y @ w_out, ssm1, xp[:, L:]


def moe(x, w_router, b_router, w_gu, b_gu, w_down, b_down):
    shp = x.shape
    xt = x.reshape(-1, D_MODEL)
    T = xt.shape[0]
    logits = (xt @ w_router).astype(F32) + b_router.astype(F32)
    top_v, top_e = lax.top_k(logits, TOP_K)
    gates = jax.nn.softmax(top_v, axis=-1)
    A = T * TOP_K
    e_flat = top_e.reshape(-1)
    tok_flat = jnp.arange(A, dtype=jnp.int32) // TOP_K
    g_flat = gates.reshape(-1)
    order = jnp.argsort(e_flat)
    e_sorted = e_flat[order]
    counts = jnp.bincount(e_flat, length=N_EXPERTS)
    padded = (counts + MOE_BLOCK - 1) // MOE_BLOCK * MOE_BLOCK
    start = jnp.cumsum(counts) - counts
    pend = jnp.cumsum(padded)
    pstart = pend - padded
    dest = pstart[e_sorted] + jnp.arange(A) - start[e_sorted]
    nb = -(-A // MOE_BLOCK) + N_EXPERTS
    slot_tok = jnp.full((nb * MOE_BLOCK,), T, jnp.int32).at[dest].set(tok_flat[order])
    slot_gate = jnp.zeros((nb * MOE_BLOCK,), F32).at[dest].set(g_flat[order])
    block_e = jnp.minimum(jnp.searchsorted(pend, jnp.arange(nb) * MOE_BLOCK, side='right'), N_EXPERTS - 1)
    x_pad = jnp.concatenate([xt, jnp.zeros((1, D_MODEL), xt.dtype)], axis=0)
    xb = x_pad[slot_tok].reshape(nb, MOE_BLOCK, D_MODEL)

    def expert_block(args):
        xe, e = args
        h = xe @ w_gu[e] + b_gu[e]
        g = jnp.minimum(h[:, :D_FF], SWIGLU_LIMIT)
        u = jnp.clip(h[:, D_FF:], -SWIGLU_LIMIT, SWIGLU_LIMIT)
        act = g * jax.nn.sigmoid(GLU_ALPHA * g) * (u + 1.0)
        return act @ w_down[e] + b_down[e]

    yb = lax.map(expert_block, (xb, block_e)).reshape(-1, D_MODEL)
    y = jnp.zeros((T + 1, D_MODEL), F32).at[slot_tok].add(yb.astype(F32) * slot_gate[:, None])
    return y[:T].reshape(shp).astype(x.dtype)


def finish_layer(x, mix, p_i, ln1_g, ln1_b, ln2_g, ln2_b, w_router, b_router, w_gu, b_gu, w_down, b_down, w_pg, w_pp):
    x = layer_norm(DEEPNORM_ALPHA * x + mix, ln1_g, ln1_b)
    x = layer_norm(DEEPNORM_ALPHA * x + moe(x, w_router, b_router, w_gu, b_gu, w_down, b_down), ln2_g, ln2_b)
    return x + jax.nn.sigmoid(x @ w_pg) * (p_i @ w_pp)


def setup_inputs(seed: int = 0) -> dict:
    key = jax.random.key(seed)
    ks = iter(jax.random.split(key, 48))

    def nrm(shape, scale):
        return jax.random.normal(next(ks), shape, F32) * scale

    n_pages = PAST_LEN // PAGE_SIZE
    n_pool = (DEC_BATCH * n_pages * 5) // 4
    x_prompt = nrm((BATCH, SEQ, D_MODEL), 1.0)
    x_sample = nrm((DEC_BATCH, DEC_SEQ, D_MODEL), 1.0)
    cache_k = nrm((N_FOX, n_pool, PAGE_SIZE, FOX_HEADS, FOX_HEAD_DIM), 1.0)
    cache_v = nrm((N_FOX, n_pool, PAGE_SIZE, FOX_HEADS, FOX_HEAD_DIM), 1.0)
    cache_logf = jax.nn.log_sigmoid(nrm((N_FOX, n_pool, PAGE_SIZE, FOX_HEADS), 1.0) + 3.5)
    state_ssm = nrm((N_SSD, DEC_BATCH, SSD_HEADS, SSD_HEAD_DIM, D_STATE), 0.3)
    state_conv = nrm((N_SSD, DEC_BATCH, CONV_K - 1, CONV_DIM), 1.0)
    page_table = jax.random.permutation(next(ks), n_pool)[:DEC_BATCH * n_pages].reshape(DEC_BATCH, n_pages).astype(jnp.int32)
    p_prompt = nrm((DEPTH, BATCH, SEQ, PLE_DIM), 1.0)
    p_sample = nrm((DEPTH, DEC_BATCH, DEC_SEQ, PLE_DIM), 1.0)
    fox_w_in = nrm((N_FOX, D_MODEL, 3 * FOX_WIDTH + FOX_HEADS), D_MODEL ** -0.5)
    fox_b_f = jnp.linspace(1.0, 6.0, FOX_HEADS, dtype=F32)[None] + nrm((N_FOX, FOX_HEADS), 0.1)
    fox_w_out = nrm((N_FOX, FOX_WIDTH, D_MODEL), FOX_WIDTH ** -0.5 * DEEPNORM_BETA)
    ssd_w_in = nrm((N_SSD, D_MODEL, SSD_IN_DIM), D_MODEL ** -0.5)
    ssd_conv_w = nrm((N_SSD, CONV_K, CONV_DIM), CONV_K ** -0.5)
    ssd_conv_b = nrm((N_SSD, CONV_DIM), 0.01)
    u = jax.random.uniform(next(ks), (N_SSD, SSD_HEADS), F32)
    dt0 = jnp.exp(u * (math.log(0.1) - math.log(0.001)) + math.log(0.001))
    ssd_dt_bias = dt0 + jnp.log(-jnp.expm1(-dt0))
    ssd_a_log = jnp.log(jax.random.uniform(next(ks), (N_SSD, SSD_HEADS), F32, minval=1.0, maxval=16.0))
    ssd_d = 1.0 + nrm((N_SSD, SSD_HEADS), 0.1)
    ssd_norm_w = 1.0 + nrm((N_SSD, D_INNER), 0.02)
    ssd_w_out = nrm((N_SSD, D_INNER, D_MODEL), D_INNER ** -0.5 * DEEPNORM_BETA)
    ln1_g = 1.0 + nrm((DEPTH, D_MODEL), 0.02)
    ln1_b = nrm((DEPTH, D_MODEL), 0.02)
    ln2_g = 1.0 + nrm((DEPTH, D_MODEL), 0.02)
    ln2_b = nrm((DEPTH, D_MODEL), 0.02)
    moe_w_router = nrm((DEPTH, D_MODEL, N_EXPERTS), D_MODEL ** -0.5)
    moe_b_router = nrm((DEPTH, N_EXPERTS), 0.01)
    moe_w_gu = nrm((DEPTH, N_EXPERTS, D_MODEL, 2 * D_FF), D_MODEL ** -0.5)
    moe_b_gu = nrm((DEPTH, N_EXPERTS, 2 * D_FF), 0.01)
    moe_w_down = nrm((DEPTH, N_EXPERTS, D_FF, D_MODEL), D_FF ** -0.5 * DEEPNORM_BETA)
    moe_b_down = nrm((DEPTH, N_EXPERTS, D_MODEL), 0.01)
    ple_w_gate = nrm((DEPTH, D_MODEL, D_MODEL), D_MODEL ** -0.5)
    ple_w_proj = nrm((DEPTH, PLE_DIM, D_MODEL), PLE_DIM ** -0.5)
    return {'x_prompt': x_prompt, 'x_sample': x_sample, 'cache_k': cache_k, 'cache_v': cache_v,
            'cache_logf': cache_logf, 'state_ssm': state_ssm, 'state_conv': state_conv,
            'page_table': page_table, 'p_prompt': p_prompt, 'p_sample': p_sample,
            'fox_w_in': fox_w_in, 'fox_b_f': fox_b_f, 'fox_w_out': fox_w_out,
            'ssd_w_in': ssd_w_in, 'ssd_conv_w': ssd_conv_w, 'ssd_conv_b': ssd_conv_b,
            'ssd_dt_bias': ssd_dt_bias, 'ssd_a_log': ssd_a_log, 'ssd_d': ssd_d,
            'ssd_norm_w': ssd_norm_w, 'ssd_w_out': ssd_w_out,
            'ln1_g': ln1_g, 'ln1_b': ln1_b, 'ln2_g': ln2_g, 'ln2_b': ln2_b,
            'moe_w_router': moe_w_router, 'moe_b_router': moe_b_router, 'moe_w_gu': moe_w_gu,
            'moe_b_gu': moe_b_gu, 'moe_w_down': moe_w_down, 'moe_b_down': moe_b_down,
            'ple_w_gate': ple_w_gate, 'ple_w_proj': ple_w_proj}


def reference(x_prompt, x_sample, cache_k, cache_v, cache_logf, state_ssm, state_conv, page_table,
              p_prompt, p_sample, fox_w_in, fox_b_f, fox_w_out, ssd_w_in, ssd_conv_w, ssd_conv_b,
              ssd_dt_bias, ssd_a_log, ssd_d, ssd_norm_w, ssd_w_out, ln1_g, ln1_b, ln2_g, ln2_b,
              moe_w_router, moe_b_router, moe_w_gu, moe_b_gu, moe_w_down, moe_b_down,
              ple_w_gate, ple_w_proj):
    xp, xs = x_prompt, x_sample
    kp, vp, lfp, hp, cp = [], [], [], [], []
    ks_, vs_, lfs, hs, cs = [], [], [], [], []
    for i in range(DEPTH):
        j = i // N_MIXERS
        if i % N_MIXERS == 0:
            mix_p, k_n, v_n, lf_n = fox_prompt(xp, fox_w_in[j], fox_b_f[j], fox_w_out[j])
            kp.append(k_n); vp.append(v_n); lfp.append(lf_n)
            mix_s, k_n, v_n, lf_n = fox_sample(xs, cache_k[j], cache_v[j], cache_logf[j], page_table,
                                               fox_w_in[j], fox_b_f[j], fox_w_out[j])
            ks_.append(k_n); vs_.append(v_n); lfs.append(lf_n)
        else:
            ssd_args = (ssd_w_in[j], ssd_conv_w[j], ssd_conv_b[j], ssd_dt_bias[j], ssd_a_log[j],
                        ssd_d[j], ssd_norm_w[j], ssd_w_out[j])
            b = xp.shape[0]
            ssm0 = jnp.zeros((b, SSD_HEADS, SSD_HEAD_DIM, D_STATE), xp.dtype)
            conv0 = jnp.zeros((b, CONV_K - 1, CONV_DIM), xp.dtype)
            mix_p, h_n, c_n = ssd_mixer(xp, ssm0, conv0, *ssd_args)
            hp.append(h_n); cp.append(c_n)
            mix_s, h_n, c_n = ssd_mixer(xs, state_ssm[j], state_conv[j], *ssd_args)
            hs.append(h_n); cs.append(c_n)
        layer_args = (ln1_g[i], ln1_b[i], ln2_g[i], ln2_b[i], moe_w_router[i], moe_b_router[i],
                      moe_w_gu[i], moe_b_gu[i], moe_w_down[i], moe_b_down[i], ple_w_gate[i], ple_w_proj[i])
        xp = finish_layer(xp, mix_p, p_prompt[i], *layer_args)
        xs = finish_layer(xs, mix_s, p_sample[i], *layer_args)
    y_prompt, y_sample = xp, xs
    new_k_prompt, new_v_prompt, new_logf_prompt = jnp.stack(kp), jnp.stack(vp), jnp.stack(lfp)
    new_ssm_prompt, new_conv_prompt = jnp.stack(hp), jnp.stack(cp)
    new_k_sample, new_v_sample, new_logf_sample = jnp.stack(ks_), jnp.stack(vs_), jnp.stack(lfs)
    new_ssm_sample, new_conv_sample = jnp.stack(hs), jnp.stack(cs)
    return (y_prompt, y_sample, new_k_prompt, new_v_prompt, new_logf_prompt, new_ssm_prompt, new_conv_prompt,
            new_k_sample, new_v_sample, new_logf_sample, new_ssm_sample, new_conv_sample)
```

```python
import functools

import jax
import jax.numpy as jnp
from jax import lax
from jax.experimental import pallas as pl
from jax.experimental.pallas import tpu as pltpu

F32 = jnp.float32
BF16 = jnp.bfloat16
I32 = jnp.int32
HI = lax.Precision.HIGHEST

D_MODEL = 1024
DEPTH = 4
PAGE_SIZE = 128
N_MIXERS = 2
FOX_HEAD_DIM = 64
FOX_HEADS = D_MODEL // FOX_HEAD_DIM
FOX_WIDTH = FOX_HEADS * FOX_HEAD_DIM
FOX_SCALE = FOX_HEAD_DIM ** -0.5
D_INNER = 2 * D_MODEL
SSD_HEAD_DIM = 64
SSD_HEADS = D_INNER // SSD_HEAD_DIM
SSD_GROUPS = 8
SSD_HEADS_PER_GROUP = SSD_HEADS // SSD_GROUPS
D_STATE = 128
CONV_K = 4
GN = SSD_GROUPS * D_STATE
CONV_DIM = D_INNER + 2 * GN
SSD_CHUNK = 128
RMS_EPS = 1e-5
N_EXPERTS = 32
TOP_K = 4
D_FF = D_MODEL
SWIGLU_LIMIT = 7.0
GLU_ALPHA = 1.702
PLE_DIM = 256
LN_EPS = 1e-5
DEEPNORM_ALPHA = (2 * DEPTH) ** 0.25

LANES = 128
SUBLANES = 8
MOE_ROWS = 256
VMEM_LIMIT = 56 * 1024 * 1024
NEG_BIG = -1e30


def _cparams(sem):
    return pltpu.CompilerParams(dimension_semantics=sem, vmem_limit_bytes=VMEM_LIMIT)


def _iota(shape, dim):
    return lax.broadcasted_iota(I32, shape, dim)


def _log_sigmoid(z):
    return jnp.minimum(z, 0.0) - jnp.log1p(jnp.exp(-jnp.abs(z)))


def _softplus(z):
    return jnp.maximum(z, 0.0) + jnp.log1p(jnp.exp(-jnp.abs(z)))


def _sigmoid(z):
    return 1.0 / (1.0 + jnp.exp(-z))


_ACT = {"log_sigmoid": _log_sigmoid, "softplus": _softplus}


def _dot_nt(a, b, **kw):
    return lax.dot_general(a, b, (((1,), (1,)), ((), ())), preferred_element_type=F32, **kw)


def _dot_tn(a, b, **kw):
    return lax.dot_general(a, b, (((0,), (0,)), ((), ())), preferred_element_type=F32, **kw)


def _mm_kernel(x_ref, w_ref, o_ref, *, scale):
    acc = jnp.dot(x_ref[...].astype(BF16), w_ref[...], preferred_element_type=F32)
    if scale != 1.0:
        acc = acc * scale
    o_ref[...] = acc.astype(o_ref.dtype)


def _mm(x, w, out_dtype, scale=1.0, tm=512, tn=1024):
    m, k = x.shape
    n = w.shape[1]
    tm, tn = min(tm, m), min(tn, n)
    return pl.pallas_call(
        functools.partial(_mm_kernel, scale=scale),
        grid=(m // tm, n // tn),
        in_specs=[pl.BlockSpec((tm, k), lambda i, j: (i, 0)),
                  pl.BlockSpec((k, tn), lambda i, j: (0, j))],
        out_specs=pl.BlockSpec((tm, tn), lambda i, j: (i, j)),
        out_shape=jax.ShapeDtypeStruct((m, n), out_dtype),
        compiler_params=_cparams(("parallel", "parallel")),
        name="mm",
    )(x, w)


def _gate_rows_kernel(x_ref, w_ref, b_ref, o_ref, *, kind):
    z = jnp.dot(x_ref[...], w_ref[...], precision=HI, preferred_element_type=F32) + b_ref[...]
    o_ref[...] = _ACT[kind](z)


def _gate_rows(x, w, b, kind, tm=512):
    m, k = x.shape
    n = w.shape[1]
    return pl.pallas_call(
        functools.partial(_gate_rows_kernel, kind=kind),
        grid=(m // tm,),
        in_specs=[pl.BlockSpec((tm, k), lambda i: (i, 0)),
                  pl.BlockSpec((k, n), lambda i: (0, 0)),
                  pl.BlockSpec((1, n), lambda i: (0, 0))],
        out_specs=pl.BlockSpec((tm, n), lambda i: (i, 0)),
        out_shape=jax.ShapeDtypeStruct((m, n), F32),
        compiler_params=_cparams(("parallel",)),
        name="gate_rows",
    )(x, w, b.reshape(1, n))


def _gate_cols_kernel(x_ref, wt_ref, b_ref, o_ref, carry_ref, *, kind, cumsum):
    z = _dot_nt(wt_ref[...], x_ref[0], precision=HI) + b_ref[...]
    y = _ACT[kind](z)
    if cumsum:
        @pl.when(pl.program_id(1) == 0)
        def _():
            carry_ref[...] = jnp.zeros_like(carry_ref)
        tl = y.shape[1]
        upper = (_iota((tl, tl), 0) <= _iota((tl, tl), 1)).astype(F32)
        cs = jnp.dot(y, upper, precision=HI, preferred_element_type=F32) + carry_ref[...]
        o_ref[0] = cs
        carry_ref[...] = cs[:, tl - 1:tl]
    else:
        o_ref[0] = y


def _gate_cols(x3, wt, b, kind, cumsum, tl=512):
    bsz, seq, k = x3.shape
    n = wt.shape[0]
    tl = min(tl, seq)
    return pl.pallas_call(
        functools.partial(_gate_cols_kernel, kind=kind, cumsum=cumsum),
        grid=(bsz, seq // tl),
        in_specs=[pl.BlockSpec((1, tl, k), lambda bi, li: (bi, li, 0)),
                  pl.BlockSpec((n, k), lambda bi, li: (0, 0)),
                  pl.BlockSpec((n, 1), lambda bi, li: (0, 0))],
        out_specs=pl.BlockSpec((1, n, tl), lambda bi, li: (bi, 0, li)),
        out_shape=jax.ShapeDtypeStruct((bsz, n, seq), F32),
        scratch_shapes=[pltpu.VMEM((n, 1), F32)],
        compiler_params=_cparams(("parallel", "arbitrary")),
        name="gate_cols",
    )(x3, wt, b.reshape(n, 1))


def _fox_attn_kernel(q_ref, k_ref, v_ref, c_ref, o_ref, kb_ref, vb_ref, *, tq):
    qi = pl.program_id(2)
    seq = k_ref.shape[1]

    @pl.when(qi == 0)
    def _():
        def cast_block(i, carry):
            rows = pl.ds(pl.multiple_of(i * tq, tq), tq)
            kb_ref[rows, :] = k_ref[0, rows, :].astype(BF16)
            vb_ref[rows, :] = v_ref[0, rows, :].astype(BF16)
            return carry
        lax.fori_loop(0, seq // tq, cast_block, 0)

    q = q_ref[0]
    lane = _iota((tq, LANES), 1)
    causal = _iota((tq, tq), 0) >= _iota((tq, tq), 1)
    outs = []
    for hh in range(2):
        qh = jnp.where((lane >= FOX_HEAD_DIM) == (hh == 1), q, jnp.zeros_like(q))

        def step(kb, carry, masked, hh=hh, qh=qh):
            m, l, acc = carry
            rows = pl.ds(pl.multiple_of(kb * tq, tq), tq)
            s = _dot_nt(qh, kb_ref[rows, :])
            s = s - c_ref[0, hh, :, rows]
            if masked:
                s = jnp.where(causal, s, -jnp.inf)
            m_new = jnp.maximum(m, jnp.max(s, axis=1, keepdims=True))
            alpha = jnp.exp(m - m_new)
            p = jnp.exp(s - m_new)
            l = alpha * l + jnp.sum(p, axis=1, keepdims=True)
            acc = alpha * acc + jnp.dot(p.astype(BF16), vb_ref[rows, :], preferred_element_type=F32)
            return m_new, l, acc

        init = (jnp.full((tq, 1), -jnp.inf, F32), jnp.zeros((tq, 1), F32), jnp.zeros((tq, LANES), F32))
        carry = lax.fori_loop(0, qi, functools.partial(step, masked=False), init)
        _, l, acc = step(qi, carry, True)
        outs.append(acc / l)
    o_ref[0] = jnp.where(lane < FOX_HEAD_DIM, outs[0], outs[1]).astype(o_ref.dtype)


def _fox_attn(q3, k3, v3, c4, tq=256):
    bsz, seq, width = q3.shape
    tq = min(tq, seq)
    pairs = width // LANES
    return pl.pallas_call(
        functools.partial(_fox_attn_kernel, tq=tq),
        grid=(bsz, pairs, seq // tq),
        in_specs=[pl.BlockSpec((1, tq, LANES), lambda b, j, i: (b, i, j)),
                  pl.BlockSpec((1, seq, LANES), lambda b, j, i: (b, 0, j)),
                  pl.BlockSpec((1, seq, LANES), lambda b, j, i: (b, 0, j)),
                  pl.BlockSpec((1, 2, 1, seq), lambda b, j, i: (b, j, 0, 0))],
        out_specs=pl.BlockSpec((1, tq, LANES), lambda b, j, i: (b, i, j)),
        out_shape=jax.ShapeDtypeStruct((bsz, seq, width), BF16),
        scratch_shapes=[pltpu.VMEM((seq, LANES), BF16), pltpu.VMEM((seq, LANES), BF16)],
        compiler_params=_cparams(("parallel", "parallel", "arbitrary")),
        name="fox_attn",
    )(q3, k3, v3, c4)


def _fox_dec_kernel(pt_ref, q_ref, kc_ref, vc_ref, lc_ref, kn_ref, vn_ref, ln_ref, o_ref,
                    qbd_ref, kpad_ref, vpad_ref, lpad_ref, m_ref, l_ref, acc_ref, carry_ref, *, n_pages):
    del pt_ref
    p = pl.program_id(1)
    ld = q_ref.shape[1]
    rows = FOX_HEADS * ld
    width = q_ref.shape[2]

    @pl.when(p == 0)
    def _():
        q = q_ref[0]
        qt = jnp.broadcast_to(q[None], (FOX_HEADS, ld, width)).reshape(rows, width)
        head_of_row = _iota((rows, width), 0) // ld
        head_of_col = _iota((rows, width), 1) // FOX_HEAD_DIM
        qbd_ref[...] = jnp.where(head_of_row == head_of_col, qt, 0.0).astype(BF16)
        m_ref[...] = jnp.full(m_ref.shape, -jnp.inf, F32)
        l_ref[...] = jnp.zeros_like(l_ref)
        acc_ref[...] = jnp.zeros_like(acc_ref)
        carry_ref[...] = jnp.zeros_like(carry_ref)

    def page(kp, vp, lf, masked):
        keys = kp.shape[0]
        s = _dot_nt(qbd_ref[...], kp.astype(BF16))
        expand = (_iota((FOX_HEADS, rows), 1) // ld == _iota((FOX_HEADS, rows), 0)).astype(F32)
        lf_rows = jnp.dot(lf, expand, precision=HI, preferred_element_type=F32)
        lower = (_iota((keys, keys), 0) >= _iota((keys, keys), 1)).astype(F32)
        c = jnp.dot(lower, lf_rows, precision=HI, preferred_element_type=F32) + carry_ref[...]
        carry_ref[...] = c[keys - 1:keys, :]
        s = s - c.T
        if masked:
            tok = _iota((rows, keys), 0) % ld
            key = _iota((rows, keys), 1)
            s = jnp.where(key <= tok, s, -jnp.inf)
        m_old = m_ref[...]
        m_new = jnp.maximum(m_old, jnp.max(s, axis=1, keepdims=True))
        alpha = jnp.exp(m_old - m_new)
        pr = jnp.exp(s - m_new)
        l_ref[...] = alpha * l_ref[...] + jnp.sum(pr, axis=1, keepdims=True)
        acc_ref[...] = alpha * acc_ref[...] + jnp.dot(pr.astype(BF16), vp.astype(BF16),
                                                      preferred_element_type=F32)
        m_ref[...] = m_new

    @pl.when(p < n_pages)
    def _():
        page(kc_ref[0, 0], vc_ref[0, 0], lc_ref[0, 0], False)

    @pl.when(p == n_pages)
    def _():
        kpad_ref[...] = jnp.zeros_like(kpad_ref)
        vpad_ref[...] = jnp.zeros_like(vpad_ref)
        lpad_ref[...] = jnp.zeros_like(lpad_ref)
        kpad_ref[0:ld, :] = kn_ref[0]
        vpad_ref[0:ld, :] = vn_ref[0]
        lpad_ref[0:ld, :] = ln_ref[0]
        page(kpad_ref[...], vpad_ref[...], lpad_ref[...], True)
        a3 = (acc_ref[...] / l_ref[...]).reshape(FOX_HEADS, ld, width)
        own = _iota((FOX_HEADS, ld, width), 0) == _iota((FOX_HEADS, ld, width), 2) // FOX_HEAD_DIM
        o_ref[0] = jnp.sum(jnp.where(own, a3, 0.0), axis=0)


def _fox_decode(q3, kn3, vn3, ln3, cache_k4, cache_v4, cache_lf4, page_table, layer):
    bd, ld, width = q3.shape
    n_pages = page_table.shape[1]
    rows = FOX_HEADS * ld

    def cache_map(b, p, pt):
        return (layer, pt[b, jnp.minimum(p, n_pages - 1)], 0, 0)

    def new_map(b, p, pt):
        return (b, 0, 0)

    grid_spec = pltpu.PrefetchScalarGridSpec(
        num_scalar_prefetch=1,
        grid=(bd, n_pages + 1),
        in_specs=[pl.BlockSpec((1, ld, width), new_map),
                  pl.BlockSpec((1, 1, PAGE_SIZE, width), cache_map),
                  pl.BlockSpec((1, 1, PAGE_SIZE, width), cache_map),
                  pl.BlockSpec((1, 1, PAGE_SIZE, FOX_HEADS), cache_map),
                  pl.BlockSpec((1, ld, width), new_map),
                  pl.BlockSpec((1, ld, width), new_map),
                  pl.BlockSpec((1, ld, FOX_HEADS), new_map)],
        out_specs=pl.BlockSpec((1, ld, width), new_map),
        scratch_shapes=[pltpu.VMEM((rows, width), BF16),
                        pltpu.VMEM((PAGE_SIZE, width), F32),
                        pltpu.VMEM((PAGE_SIZE, width), F32),
                        pltpu.VMEM((PAGE_SIZE, FOX_HEADS), F32),
                        pltpu.VMEM((rows, 1), F32),
                        pltpu.VMEM((rows, 1), F32),
                        pltpu.VMEM((rows, width), F32),
                        pltpu.VMEM((1, rows), F32)])
    return pl.pallas_call(
        functools.partial(_fox_dec_kernel, n_pages=n_pages),
        grid_spec=grid_spec,
        out_shape=jax.ShapeDtypeStruct((bd, ld, width), F32),
        compiler_params=_cparams(("parallel", "arbitrary")),
        name="fox_decode",
    )(page_table, q3, cache_k4, cache_v4, cache_lf4, kn3, vn3, ln3)


def _ssd_chunk_kernel(xbc_ref, z_ref, dt_ref, dtt_ref, cw_ref, cb_ref, al_ref, alt_ref, dsk_ref, nw_ref,
                      y_ref, st_ref, xp_ref, act_ref, ysc_ref, state_ref):
    c = pl.program_id(1)
    nc = pl.num_programs(1)
    q = SSD_CHUNK
    pair_w = 2 * SSD_HEAD_DIM

    @pl.when(c == 0)
    def _():
        state_ref[...] = jnp.zeros_like(state_ref)
        xp_ref[0:SUBLANES, :] = jnp.zeros((SUBLANES, CONV_DIM), F32)

    xp_ref[SUBLANES:SUBLANES + q, :] = xbc_ref[0]
    conv = cb_ref[...]
    for t in range(CONV_K):
        lo = SUBLANES - (CONV_K - 1) + t
        conv = conv + xp_ref[lo:lo + q, :] * cw_ref[t:t + 1, :]
    xp_ref[0:SUBLANES, :] = xp_ref[q:q + SUBLANES, :]
    act_ref[...] = conv * _sigmoid(conv)

    dt = dt_ref[0]
    dtt = dtt_ref[0]
    a = dt * (-jnp.exp(al_ref[...]))
    at = dtt * (-jnp.exp(alt_ref[...]))
    lower = (_iota((q, q), 0) >= _iota((q, q), 1))
    acum = jnp.dot(lower.astype(F32), a, precision=HI, preferred_element_type=F32)
    acum_t = jnp.dot(at, (_iota((q, q), 0) <= _iota((q, q), 1)).astype(F32), precision=HI,
                     preferred_element_type=F32)
    a_end = acum[q - 1:q, :]
    decay_end = jnp.exp(a_end - acum) * dt
    exp_acum = jnp.exp(acum)
    chunk_decay = jnp.exp(acum_t[:, q - 1:q])

    lane = _iota((q, pair_w), 1)
    first = lane < SSD_HEAD_DIM
    row_first = _iota((pair_w, 1), 0) < SSD_HEAD_DIM
    for g in range(SSD_GROUPS):
        bg = act_ref[:, D_INNER + g * D_STATE:D_INNER + (g + 1) * D_STATE].astype(BF16)
        cg = act_ref[:, D_INNER + GN + g * D_STATE:D_INNER + GN + (g + 1) * D_STATE].astype(BF16)
        cb = _dot_nt(cg, bg)
        for pr in range(SSD_HEADS_PER_GROUP // 2):
            pi = g * (SSD_HEADS_PER_GROUP // 2) + pr
            h0 = 2 * pi
            xpair = act_ref[:, pi * pair_w:(pi + 1) * pair_w]
            xpair_b = xpair.astype(BF16)
            ys = []
            for hh in range(2):
                h = h0 + hh
                seg = acum[:, h:h + 1] - acum_t[h:h + 1, :]
                w = cb * jnp.exp(jnp.where(lower, seg, -jnp.inf)) * dtt[h:h + 1, :]
                ys.append(jnp.dot(w.astype(BF16), xpair_b, preferred_element_type=F32))
            y_diag = jnp.where(first, ys[0], ys[1])
            s_pair = state_ref[pi]
            ea = jnp.where(first, exp_acum[:, h0:h0 + 1], exp_acum[:, h0 + 1:h0 + 2])
            y_off = _dot_nt(cg, s_pair.astype(BF16)) * ea
            ysc_ref[:, pi * pair_w:(pi + 1) * pair_w] = y_diag + y_off
            de = jnp.where(first, decay_end[:, h0:h0 + 1], decay_end[:, h0 + 1:h0 + 2])
            contrib = _dot_tn((xpair * de).astype(BF16), bg)
            cd = jnp.where(row_first, chunk_decay[h0:h0 + 1, :], chunk_decay[h0 + 1:h0 + 2, :])
            state_ref[pi] = s_pair * cd + contrib

    gw = D_INNER // SSD_GROUPS
    for g in range(SSD_GROUPS):
        cols = slice(g * gw, (g + 1) * gw)
        zg = z_ref[0, :, cols]
        yg = (ysc_ref[:, cols] + dsk_ref[:, cols] * act_ref[:, cols]) * (zg * _sigmoid(zg))
        ms = jnp.mean(yg * yg, axis=1, keepdims=True)
        y_ref[0, :, cols] = (yg * lax.rsqrt(ms + RMS_EPS) * nw_ref[:, cols]).astype(y_ref.dtype)

    @pl.when(c == nc - 1)
    def _():
        st_ref[0] = state_ref[...]


def _ssd_prompt(xbc3, z3, dt3, dtt3, conv_w, conv_b, a_log, d_lanes, norm_w):
    bsz, seq, _ = xbc3.shape
    q = SSD_CHUNK
    n_pairs = SSD_HEADS // 2
    const2 = lambda b, c: (0, 0)
    y, st = pl.pallas_call(
        _ssd_chunk_kernel,
        grid=(bsz, seq // q),
        in_specs=[pl.BlockSpec((1, q, CONV_DIM), lambda b, c: (b, c, 0)),
                  pl.BlockSpec((1, q, D_INNER), lambda b, c: (b, c, 0)),
                  pl.BlockSpec((1, q, SSD_HEADS), lambda b, c: (b, c, 0)),
                  pl.BlockSpec((1, SSD_HEADS, q), lambda b, c: (b, 0, c)),
                  pl.BlockSpec((CONV_K, CONV_DIM), const2),
                  pl.BlockSpec((1, CONV_DIM), const2),
                  pl.BlockSpec((1, SSD_HEADS), const2),
                  pl.BlockSpec((SSD_HEADS, 1), const2),
                  pl.BlockSpec((1, D_INNER), const2),
                  pl.BlockSpec((1, D_INNER), const2)],
        out_specs=[pl.BlockSpec((1, q, D_INNER), lambda b, c: (b, c, 0)),
                   pl.BlockSpec((1, n_pairs, 2 * SSD_HEAD_DIM, D_STATE), lambda b, c: (b, 0, 0, 0))],
        out_shape=[jax.ShapeDtypeStruct((bsz, seq, D_INNER), BF16),
                   jax.ShapeDtypeStruct((bsz, n_pairs, 2 * SSD_HEAD_DIM, D_STATE), F32)],
        scratch_shapes=[pltpu.VMEM((q + SUBLANES, CONV_DIM), F32),
                        pltpu.VMEM((q, CONV_DIM), F32),
                        pltpu.VMEM((q, D_INNER), F32),
                        pltpu.VMEM((n_pairs, 2 * SSD_HEAD_DIM, D_STATE), F32)],
        compiler_params=_cparams(("parallel", "arbitrary")),
        name="ssd_prompt",
    )(xbc3, z3, dt3, dtt3, conv_w, conv_b.reshape(1, CONV_DIM), a_log.reshape(1, SSD_HEADS),
      a_log.reshape(SSD_HEADS, 1), d_lanes, norm_w.reshape(1, D_INNER))
    return y, st.reshape(bsz, SSD_HEADS, SSD_HEAD_DIM, D_STATE)


def _ssd_step_kernel(xbc_ref, c0_ref, z_ref, dt_ref, s0_ref, cw_ref, cb_ref, al_ref, dsk_ref, nw_ref,
                     hexp_ref, gsum_ref, y_ref, s1_ref, xp_ref, xd_ref, bpad_ref):
    ld = xbc_ref.shape[1]
    tail = CONV_K - 1
    pair_w = 2 * SSD_HEAD_DIM
    gw = D_INNER // SSD_GROUPS

    xp_ref[SUBLANES - tail:SUBLANES, :] = c0_ref[0]
    xp_ref[SUBLANES:SUBLANES + ld, :] = xbc_ref[0]
    conv = cb_ref[...]
    for t in range(CONV_K):
        lo = SUBLANES - tail + t
        conv = conv + xp_ref[lo:lo + ld, :] * cw_ref[t:t + 1, :]
    act = conv * _sigmoid(conv)
    xs = act[:, :D_INNER]
    bm = act[:, D_INNER:D_INNER + GN]
    cm = act[:, D_INNER + GN:]

    dt = dt_ref[0]
    a = dt * (-jnp.exp(al_ref[...]))
    row = _iota((ld, SSD_HEADS), 0)
    acum = a
    sh = 1
    while sh < ld:
        acum = acum + jnp.where(row >= sh, pltpu.roll(acum, sh, 0), 0.0)
        sh *= 2
    a_end = acum[ld - 1:ld, :]
    hexp = hexp_ref[...]

    prod = jnp.concatenate([cm * bm[s:s + 1, :] for s in range(ld)], axis=0)
    cbh = jnp.dot(prod, gsum_ref[...], precision=HI, preferred_element_type=F32)
    a_l = jnp.concatenate([acum] * ld, axis=0)
    a_s = jnp.concatenate([jnp.broadcast_to(acum[s:s + 1, :], (ld, SSD_HEADS)) for s in range(ld)], axis=0)
    dt_s = jnp.concatenate([jnp.broadcast_to(dt[s:s + 1, :], (ld, SSD_HEADS)) for s in range(ld)], axis=0)
    pr_row = _iota((ld * ld, SSD_HEADS), 0)
    causal = (pr_row % ld) >= (pr_row // ld)
    w = cbh * jnp.exp(jnp.where(causal, a_l - a_s, -jnp.inf)) * dt_s
    wexp = jnp.dot(w, hexp, precision=HI, preferred_element_type=F32)
    y = jnp.zeros((ld, D_INNER), F32)
    for s in range(ld):
        y = y + wexp[s * ld:(s + 1) * ld, :] * xs[s:s + 1, :]

    ea = jnp.dot(jnp.exp(acum), hexp, precision=HI, preferred_element_type=F32)
    de = jnp.dot(jnp.exp(a_end - acum) * dt, hexp, precision=HI, preferred_element_type=F32)
    xd_ref[...] = jnp.zeros_like(xd_ref)
    bpad_ref[...] = jnp.zeros_like(bpad_ref)
    xd_ref[0:ld, :] = xs * de
    bpad_ref[0:ld, :] = bm
    chunk_decay = jnp.exp(a_end)
    pairs_per_group = SSD_HEADS_PER_GROUP // 2
    y_off = []
    for g in range(SSD_GROUPS):
        cg = cm[:, g * D_STATE:(g + 1) * D_STATE].astype(BF16)
        bg = bpad_ref[:, g * D_STATE:(g + 1) * D_STATE].astype(BF16)
        for pr in range(pairs_per_group):
            pi = g * pairs_per_group + pr
            s_pair = s0_ref[0, pi]
            y_off.append(_dot_nt(cg, s_pair.astype(BF16)))
            contrib = _dot_tn(xd_ref[:, pi * pair_w:(pi + 1) * pair_w].astype(BF16), bg)
            for hh in range(2):
                h = 2 * pi + hh
                rows = slice(hh * SSD_HEAD_DIM, (hh + 1) * SSD_HEAD_DIM)
                s1_ref[0, pi, rows, :] = s_pair[rows, :] * chunk_decay[0, h] + contrib[rows, :]
    y = y + jnp.concatenate(y_off, axis=1) * ea

    y = (y + dsk_ref[...] * xs) * (z_ref[0] * _sigmoid(z_ref[0]))
    for g in range(SSD_GROUPS):
        cols = slice(g * gw, (g + 1) * gw)
        yg = y[:, cols]
        ms = jnp.mean(yg * yg, axis=1, keepdims=True)
        y_ref[0, :, cols] = yg * lax.rsqrt(ms + RMS_EPS) * nw_ref[:, cols]


def _ssd_sample(xbc3, conv0, z3, dt3, state0, conv_w, conv_b, a_log, d_lanes, norm_w):
    bd, ld, _ = xbc3.shape
    n_pairs = SSD_HEADS // 2
    pair_w = 2 * SSD_HEAD_DIM
    hexp = (jnp.arange(D_INNER, dtype=I32)[None, :] // SSD_HEAD_DIM == jnp.arange(SSD_HEADS, dtype=I32)[:, None]).astype(F32)
    gsum = (jnp.arange(GN, dtype=I32)[:, None] // D_STATE ==
            jnp.arange(SSD_HEADS, dtype=I32)[None, :] // SSD_HEADS_PER_GROUP).astype(F32)
    s0 = state0.reshape(bd, n_pairs, pair_w, D_STATE)
    const2 = lambda b: (0, 0)
    y, s1 = pl.pallas_call(
        _ssd_step_kernel,
        grid=(bd,),
        in_specs=[pl.BlockSpec((1, ld, CONV_DIM), lambda b: (b, 0, 0)),
                  pl.BlockSpec((1, CONV_K - 1, CONV_DIM), lambda b: (b, 0, 0)),
                  pl.BlockSpec((1, ld, D_INNER), lambda b: (b, 0, 0)),
                  pl.BlockSpec((1, ld, SSD_HEADS), lambda b: (b, 0, 0)),
                  pl.BlockSpec((1, n_pairs, pair_w, D_STATE), lambda b: (b, 0, 0, 0)),
                  pl.BlockSpec((CONV_K, CONV_DIM), const2),
                  pl.BlockSpec((1, CONV_DIM), const2),
                  pl.BlockSpec((1, SSD_HEADS), const2),
                  pl.BlockSpec((1, D_INNER), const2),
                  pl.BlockSpec((1, D_INNER), const2),
                  pl.BlockSpec((SSD_HEADS, D_INNER), const2),
                  pl.BlockSpec((GN, SSD_HEADS), const2)],
        out_specs=[pl.BlockSpec((1, ld, D_INNER), lambda b: (b, 0, 0)),
                   pl.BlockSpec((1, n_pairs, pair_w, D_STATE), lambda b: (b, 0, 0, 0))],
        out_shape=[jax.ShapeDtypeStruct((bd, ld, D_INNER), F32),
                   jax.ShapeDtypeStruct((bd, n_pairs, pair_w, D_STATE), F32)],
        scratch_shapes=[pltpu.VMEM((SUBLANES + ld, CONV_DIM), F32),
                        pltpu.VMEM((LANES, D_INNER), F32),
                        pltpu.VMEM((LANES, GN), F32)],
        compiler_params=_cparams(("parallel",)),
        name="ssd_sample",
    )(xbc3, conv0, z3, dt3, s0, conv_w, conv_b.reshape(1, CONV_DIM), a_log.reshape(1, SSD_HEADS),
      d_lanes, norm_w.reshape(1, D_INNER), hexp, gsum)
    return y, s1.reshape(bd, SSD_HEADS, SSD_HEAD_DIM, D_STATE)


def _layer_norm(y, g, b):
    mu = jnp.mean(y, axis=1, keepdims=True)
    d = y - mu
    var = jnp.mean(d * d, axis=1, keepdims=True)
    return d * lax.rsqrt(var + LN_EPS) * g + b


def _post_mix_kernel(o_ref, w_ref, x_ref, g_ref, b_ref, wr_ref, br_ref, x1_ref, x1b_ref, te_ref, tg_ref):
    mix = jnp.dot(o_ref[...].astype(BF16), w_ref[...], preferred_element_type=F32)
    x1 = _layer_norm(DEEPNORM_ALPHA * x_ref[...] + mix, g_ref[...], b_ref[...])
    x1_ref[...] = x1
    x1b_ref[...] = x1.astype(BF16)
    logits = jnp.dot(x1, wr_ref[...], precision=HI, preferred_element_type=F32) + br_ref[...]
    lane = _iota(logits.shape, 1)
    vals, idxs = [], []
    cur = logits
    for _ in range(TOP_K):
        mx = jnp.max(cur, axis=1, keepdims=True)
        idx = jnp.min(jnp.where(cur == mx, lane, LANES), axis=1, keepdims=True)
        vals.append(mx)
        idxs.append(idx)
        cur = jnp.where(lane == idx, -jnp.inf, cur)
    ex = [jnp.exp(v - vals[0]) for v in vals]
    den = ex[0] + ex[1] + ex[2] + ex[3]
    te = jnp.zeros(logits.shape, I32)
    tg = jnp.zeros(logits.shape, F32)
    for k in range(TOP_K):
        te = jnp.where(lane == k, idxs[k], te)
        tg = jnp.where(lane == k, ex[k] / den, tg)
    te_ref[...] = te
    tg_ref[...] = tg


def _post_mix(o, w_out, x, g, b, w_router, b_router, tm=256):
    m, k = o.shape
    wr = jnp.zeros((D_MODEL, LANES), F32).at[:, :N_EXPERTS].set(w_router)
    br = jnp.full((1, LANES), NEG_BIG, F32).at[0, :N_EXPERTS].set(b_router)
    row = lambda i: (i, 0)
    const = lambda i: (0, 0)
    return pl.pallas_call(
        _post_mix_kernel,
        grid=(m // tm,),
        in_specs=[pl.BlockSpec((tm, k), row),
                  pl.BlockSpec((k, D_MODEL), const),
                  pl.BlockSpec((tm, D_MODEL), row),
                  pl.BlockSpec((1, D_MODEL), const),
                  pl.BlockSpec((1, D_MODEL), const),
                  pl.BlockSpec((D_MODEL, LANES), const),
                  pl.BlockSpec((1, LANES), const)],
        out_specs=[pl.BlockSpec((tm, D_MODEL), row), pl.BlockSpec((tm, D_MODEL), row),
                   pl.BlockSpec((tm, LANES), row), pl.BlockSpec((tm, LANES), row)],
        out_shape=[jax.ShapeDtypeStruct((m, D_MODEL), F32), jax.ShapeDtypeStruct((m, D_MODEL), BF16),
                   jax.ShapeDtypeStruct((m, LANES), I32), jax.ShapeDtypeStruct((m, LANES), F32)],
        compiler_params=_cparams(("parallel",)),
        name="post_mix",
    )(o, w_out, x, g.reshape(1, D_MODEL), b.reshape(1, D_MODEL), wr, br)


def _rank_kernel(te_ref, rank_ref, cnt_ref, carry_ref):
    @pl.when(pl.program_id(0) == 0)
    def _():
        carry_ref[...] = jnp.zeros_like(carry_ref)
    te = te_ref[...]
    tr = te.shape[0]
    lane = _iota((tr, LANES), 1)
    onehot = jnp.zeros((tr, LANES), F32)
    for k in range(TOP_K):
        onehot = onehot + (lane == te[:, k:k + 1]).astype(F32)
    strict = (_iota((tr, tr), 0) > _iota((tr, tr), 1)).astype(BF16)
    before = jnp.dot(strict, onehot.astype(BF16), preferred_element_type=F32) + carry_ref[0:1, :]
    out = jnp.zeros((tr, LANES), F32)
    for k in range(TOP_K):
        rk = jnp.sum(jnp.where(lane == te[:, k:k + 1], before, 0.0), axis=1, keepdims=True)
        out = jnp.where(lane == k, rk, out)
    rank_ref[...] = out.astype(I32)
    total = carry_ref[0:1, :] + jnp.sum(onehot, axis=0, keepdims=True)
    carry_ref[...] = jnp.broadcast_to(total, carry_ref.shape)
    cnt_ref[...] = jnp.broadcast_to(total, cnt_ref.shape).astype(I32)


def _route_ranks(te, tr=512):
    m = te.shape[0]
    return pl.pallas_call(
        _rank_kernel,
        grid=(m // tr,),
        in_specs=[pl.BlockSpec((tr, LANES), lambda i: (i, 0))],
        out_specs=[pl.BlockSpec((tr, LANES), lambda i: (i, 0)),
                   pl.BlockSpec((SUBLANES, LANES), lambda i: (0, 0))],
        out_shape=[jax.ShapeDtypeStruct((m, LANES), I32), jax.ShapeDtypeStruct((SUBLANES, LANES), I32)],
        scratch_shapes=[pltpu.VMEM((SUBLANES, LANES), F32)],
        compiler_params=_cparams(("arbitrary",)),
        name="route_ranks",
    )(te)


def _expert_kernel(be_ref, nu_ref, x_ref, wgu_ref, bgu_ref, wd_ref, bd_ref, o_ref, wgu_b, wd_b):
    n = pl.program_id(0)
    prev = be_ref[jnp.maximum(n - 1, 0)]
    used = n < nu_ref[0]

    @pl.when(jnp.logical_and(used, jnp.logical_or(n == 0, be_ref[n] != prev)))
    def _():
        wgu_b[...] = wgu_ref[0, 0].astype(BF16)
        wd_b[...] = wd_ref[0, 0].astype(BF16)

    @pl.when(used)
    def _():
        h = jnp.dot(x_ref[...], wgu_b[...], preferred_element_type=F32) + bgu_ref[0, 0]
        g = jnp.minimum(h[:, :D_FF], SWIGLU_LIMIT)
        u = jnp.clip(h[:, D_FF:], -SWIGLU_LIMIT, SWIGLU_LIMIT)
        act = g * _sigmoid(GLU_ALPHA * g) * (u + 1.0)
        o_ref[...] = jnp.dot(act.astype(BF16), wd_b[...], preferred_element_type=F32) + bd_ref[0, 0]

    @pl.when(jnp.logical_not(used))
    def _():
        o_ref[...] = jnp.zeros_like(o_ref)


def _experts(xb, block_e, n_used, w_gu, b_gu, w_down, b_down, layer):
    rows = xb.shape[0]
    nb = rows // MOE_ROWS

    def wmap(n, be, nu):
        return (layer, be[n], 0, 0)

    grid_spec = pltpu.PrefetchScalarGridSpec(
        num_scalar_prefetch=2,
        grid=(nb,),
        in_specs=[pl.BlockSpec((MOE_ROWS, D_MODEL), lambda n, be, nu: (n, 0)),
                  pl.BlockSpec((1, 1, D_MODEL, 2 * D_FF), wmap),
                  pl.BlockSpec((1, 1, 1, 2 * D_FF), wmap),
                  pl.BlockSpec((1, 1, D_FF, D_MODEL), wmap),
                  pl.BlockSpec((1, 1, 1, D_MODEL), wmap)],
        out_specs=pl.BlockSpec((MOE_ROWS, D_MODEL), lambda n, be, nu: (n, 0)),
        scratch_shapes=[pltpu.VMEM((D_MODEL, 2 * D_FF), BF16), pltpu.VMEM((D_FF, D_MODEL), BF16)])
    return pl.pallas_call(
        _expert_kernel,
        grid_spec=grid_spec,
        out_shape=jax.ShapeDtypeStruct((rows, D_MODEL), F32),
        compiler_params=_cparams(("arbitrary",)),
        name="experts",
    )(block_e, n_used, xb, w_gu, b_gu.reshape(DEPTH, N_EXPERTS, 1, 2 * D_FF), w_down,
      b_down.reshape(DEPTH, N_EXPERTS, 1, D_MODEL))


def _finish_kernel(x1_ref, yk_ref, tg_ref, g_ref, b_ref, wg_ref, p_ref, wp_ref, o_ref):
    tg = tg_ref[...]
    moe = jnp.zeros(x1_ref.shape, F32)
    for k in range(TOP_K):
        moe = moe + tg[:, k:k + 1] * yk_ref[k]
    x2 = _layer_norm(DEEPNORM_ALPHA * x1_ref[...] + moe, g_ref[...], b_ref[...])
    gate = _sigmoid(jnp.dot(x2.astype(BF16), wg_ref[...], preferred_element_type=F32))
    proj = jnp.dot(p_ref[...].astype(BF16), wp_ref[...], preferred_element_type=F32)
    o_ref[...] = x2 + gate * proj


def _finish(x1, yk, tg, g, b, w_pg, p, w_pp, tm=256):
    m = x1.shape[0]
    row = lambda i: (i, 0)
    const = lambda i: (0, 0)
    return pl.pallas_call(
        _finish_kernel,
        grid=(m // tm,),
        in_specs=[pl.BlockSpec((tm, D_MODEL), row),
                  pl.BlockSpec((TOP_K, tm, D_MODEL), lambda i: (0, i, 0)),
                  pl.BlockSpec((tm, LANES), row),
                  pl.BlockSpec((1, D_MODEL), const),
                  pl.BlockSpec((1, D_MODEL), const),
                  pl.BlockSpec((D_MODEL, D_MODEL), const),
                  pl.BlockSpec((tm, PLE_DIM), row),
                  pl.BlockSpec((PLE_DIM, D_MODEL), const)],
        out_specs=pl.BlockSpec((tm, D_MODEL), row),
        out_shape=jax.ShapeDtypeStruct((m, D_MODEL), F32),
        compiler_params=_cparams(("parallel",)),
        name="finish",
    )(x1, yk, tg, g.reshape(1, D_MODEL), b.reshape(1, D_MODEL), w_pg, p, w_pp)


def _moe_and_finish(layer, x1, x1b, te, tg, p_all, ln2_g, ln2_b, w_gu, b_gu, w_down, b_down, w_pg, w_pp):
    t = x1.shape[0]
    a = t * TOP_K
    rank, cnt = _route_ranks(te)
    counts = cnt[0, :N_EXPERTS]
    padded = (counts + MOE_ROWS - 1) // MOE_ROWS * MOE_ROWS
    pend = jnp.cumsum(padded)
    pstart = pend - padded
    e4 = te[:, :TOP_K]
    dest = pstart[e4] + rank[:, :TOP_K]
    nb = a // MOE_ROWS + N_EXPERTS
    tok = jnp.broadcast_to(jnp.arange(t, dtype=I32)[:, None], (t, TOP_K))
    slot_tok = jnp.zeros((nb * MOE_ROWS,), I32).at[dest.reshape(-1)].set(tok.reshape(-1))
    block_e = jnp.minimum(jnp.searchsorted(pend, jnp.arange(nb, dtype=I32) * MOE_ROWS, side="right"),
                          N_EXPERTS - 1).astype(I32)
    n_used = (pend[-1:] // MOE_ROWS).astype(I32)
    xb = x1b[slot_tok]
    yb = _experts(xb, block_e, n_used, w_gu, b_gu, w_down, b_down, layer)
    yk = yb[dest.T]
    return _finish(x1, yk, tg, ln2_g, ln2_b, w_pg, p_all, w_pp)


def kernel(x_prompt, x_sample, cache_k, cache_v, cache_logf, state_ssm, state_conv, page_table, p_prompt, p_sample,
           fox_w_in, fox_b_f, fox_w_out, ssd_w_in, ssd_conv_w, ssd_conv_b, ssd_dt_bias, ssd_a_log, ssd_d,
           ssd_norm_w, ssd_w_out, ln1_g, ln1_b, ln2_g, ln2_b, moe_w_router, moe_b_router, moe_w_gu, moe_b_gu,
           moe_w_down, moe_b_down, ple_w_gate, ple_w_proj):
    bsz, seq, _ = x_prompt.shape
    bd, ld, _ = x_sample.shape
    tp = bsz * seq
    ts = bd * ld
    n_fox = cache_k.shape[0]
    n_pool = cache_k.shape[1]
    x = jnp.concatenate([x_prompt.reshape(tp, D_MODEL), x_sample.reshape(ts, D_MODEL)], axis=0)
    p_all = jnp.concatenate([p_prompt.reshape(DEPTH, tp, PLE_DIM), p_sample.reshape(DEPTH, ts, PLE_DIM)], axis=1)
    cache_k4 = cache_k.reshape(n_fox, n_pool, PAGE_SIZE, FOX_WIDTH)
    cache_v4 = cache_v.reshape(n_fox, n_pool, PAGE_SIZE, FOX_WIDTH)

    kp, vp, lfp, hp, cp = [], [], [], [], []
    ks_, vs_, lfs, hs, cs = [], [], [], [], []
    for i in range(DEPTH):
        j = i // N_MIXERS
        if i % N_MIXERS == 0:
            w_in = fox_w_in[j]
            wq = w_in[:, :FOX_WIDTH].astype(BF16)
            wk = w_in[:, FOX_WIDTH:2 * FOX_WIDTH].astype(BF16)
            wv = w_in[:, 2 * FOX_WIDTH:3 * FOX_WIDTH].astype(BF16)
            wf = w_in[:, 3 * FOX_WIDTH:]
            k_all = _mm(x, wk, F32)
            v_all = _mm(x, wv, F32)
            lf_all = _gate_rows(x, wf, fox_b_f[j], "log_sigmoid")
            x_p3 = x[:tp].reshape(bsz, seq, D_MODEL)
            q_p = _mm(x[:tp], wq, BF16, scale=FOX_SCALE).reshape(bsz, seq, FOX_WIDTH)
            q_s = _mm(x[tp:], wq, F32, scale=FOX_SCALE).reshape(bd, ld, FOX_WIDTH)
            c4 = _gate_cols(x_p3, wf.T, fox_b_f[j], "log_sigmoid", True).reshape(bsz, FOX_HEADS, 1, seq)
            k_p3 = k_all[:tp].reshape(bsz, seq, FOX_WIDTH)
            v_p3 = v_all[:tp].reshape(bsz, seq, FOX_WIDTH)
            k_s3 = k_all[tp:].reshape(bd, ld, FOX_WIDTH)
            v_s3 = v_all[tp:].reshape(bd, ld, FOX_WIDTH)
            lf_s3 = lf_all[tp:].reshape(bd, ld, FOX_HEADS)
            o_p = _fox_attn(q_p, k_p3, v_p3, c4)
            o_s = _fox_decode(q_s, k_s3, v_s3, lf_s3, cache_k4, cache_v4, cache_logf, page_table, j)
            o_all = jnp.concatenate([o_p.reshape(tp, FOX_WIDTH), o_s.reshape(ts, FOX_WIDTH).astype(BF16)], axis=0)
            w_out = fox_w_out[j].astype(BF16)
            kp.append(k_p3.reshape(bsz, seq, FOX_HEADS, FOX_HEAD_DIM))
            vp.append(v_p3.reshape(bsz, seq, FOX_HEADS, FOX_HEAD_DIM))
            lfp.append(lf_all[:tp].reshape(bsz, seq, FOX_HEADS))
            ks_.append(k_s3.reshape(bd, ld, FOX_HEADS, FOX_HEAD_DIM))
            vs_.append(v_s3.reshape(bd, ld, FOX_HEADS, FOX_HEAD_DIM))
            lfs.append(lf_s3)
        else:
            w_in = ssd_w_in[j]
            wz = w_in[:, :D_INNER].astype(BF16)
            wx = w_in[:, D_INNER:D_INNER + CONV_DIM].astype(BF16)
            wd = w_in[:, D_INNER + CONV_DIM:]
            z_all = _mm(x, wz, F32)
            xbc_all = _mm(x, wx, F32)
            dt_all = _gate_rows(x, wd, ssd_dt_bias[j], "softplus")
            x_p3 = x[:tp].reshape(bsz, seq, D_MODEL)
            dtt = _gate_cols(x_p3, wd.T, ssd_dt_bias[j], "softplus", False)
            d_lanes = jnp.repeat(ssd_d[j], SSD_HEAD_DIM).reshape(1, D_INNER)
            xbc_p3 = xbc_all[:tp].reshape(bsz, seq, CONV_DIM)
            xbc_s3 = xbc_all[tp:].reshape(bd, ld, CONV_DIM)
            y_p, h_p = _ssd_prompt(xbc_p3, z_all[:tp].reshape(bsz, seq, D_INNER),
                                   dt_all[:tp].reshape(bsz, seq, SSD_HEADS), dtt,
                                   ssd_conv_w[j], ssd_conv_b[j], ssd_a_log[j], d_lanes, ssd_norm_w[j])
            y_s, h_s = _ssd_sample(xbc_s3, state_conv[j], z_all[tp:].reshape(bd, ld, D_INNER),
                                   dt_all[tp:].reshape(bd, ld, SSD_HEADS), state_ssm[j],
                                   ssd_conv_w[j], ssd_conv_b[j], ssd_a_log[j], d_lanes, ssd_norm_w[j])
            o_all = jnp.concatenate([y_p.reshape(tp, D_INNER), y_s.reshape(ts, D_INNER).astype(BF16)], axis=0)
            w_out = ssd_w_out[j].astype(BF16)
            tail = CONV_K - 1
            hp.append(h_p)
            cp.append(xbc_p3[:, seq - tail:, :])
            hs.append(h_s)
            cs.append(jnp.concatenate([state_conv[j], xbc_s3], axis=1)[:, ld:, :])
        x1, x1b, te, tg = _post_mix(o_all, w_out, x, ln1_g[i], ln1_b[i], moe_w_router[i], moe_b_router[i])
        x = _moe_and_finish(i, x1, x1b, te, tg, p_all[i], ln2_g[i], ln2_b[i], moe_w_gu, moe_b_gu,
                            moe_w_down, moe_b_down, ple_w_gate[i].astype(BF16), ple_w_proj[i].astype(BF16))
    y_prompt = x[:tp].reshape(bsz, seq, D_MODEL)
    y_sample = x[tp:].reshape(bd, ld, D_MODEL)
    return (y_prompt, y_sample, jnp.stack(kp), jnp.stack(vp), jnp.stack(lfp), jnp.stack(hp), jnp.stack(cp),
            jnp.stack(ks_), jnp.stack(vs_), jnp.stack(lfs), jnp.stack(hs), jnp.stack(cs))
```

```python
import functools

import jax
import jax.numpy as jnp
from jax import lax
from jax.experimental import pallas as pl
from jax.experimental.pallas import tpu as pltpu

F32 = jnp.float32
BF16 = jnp.bfloat16
I32 = jnp.int32
HI = lax.Precision.HIGHEST

D_MODEL = 1024
DEPTH = 4
PAGE_SIZE = 128
N_MIXERS = 2
FOX_HEAD_DIM = 64
FOX_HEADS = D_MODEL // FOX_HEAD_DIM
FOX_WIDTH = FOX_HEADS * FOX_HEAD_DIM
FOX_SCALE = FOX_HEAD_DIM ** -0.5
D_INNER = 2 * D_MODEL
SSD_HEAD_DIM = 64
SSD_HEADS = D_INNER // SSD_HEAD_DIM
SSD_GROUPS = 8
SSD_HEADS_PER_GROUP = SSD_HEADS // SSD_GROUPS
D_STATE = 128
CONV_K = 4
GN = SSD_GROUPS * D_STATE
CONV_DIM = D_INNER + 2 * GN
SSD_CHUNK = 128
RMS_EPS = 1e-5
N_EXPERTS = 32
TOP_K = 4
D_FF = D_MODEL
SWIGLU_LIMIT = 7.0
GLU_ALPHA = 1.702
PLE_DIM = 256
LN_EPS = 1e-5
DEEPNORM_ALPHA = (2 * DEPTH) ** 0.25

LANES = 128
SUBLANES = 8
MOE_ROWS = 256
VMEM_LIMIT = 56 * 1024 * 1024
NEG_BIG = -1e30
LOG2E = 1.4426950408889634
DECODE_PAGES_PER_STEP = 4
BIAS_PAGES_PER_STEP = 8


def _cparams(sem):
    return pltpu.CompilerParams(dimension_semantics=sem, vmem_limit_bytes=VMEM_LIMIT)


def _iota(shape, dim):
    return lax.broadcasted_iota(I32, shape, dim)


def _log_sigmoid(z):
    return jnp.minimum(z, 0.0) - jnp.log1p(jnp.exp(-jnp.abs(z)))


def _softplus(z):
    return jnp.maximum(z, 0.0) + jnp.log1p(jnp.exp(-jnp.abs(z)))


def _sigmoid(z):
    return 1.0 / (1.0 + jnp.exp(-z))


_ACT = {"log_sigmoid": _log_sigmoid, "softplus": _softplus}


def _dot_nt(a, b, **kw):
    return lax.dot_general(a, b, (((1,), (1,)), ((), ())), preferred_element_type=F32, **kw)


def _dot_tn(a, b, **kw):
    return lax.dot_general(a, b, (((0,), (0,)), ((), ())), preferred_element_type=F32, **kw)


def _mm_kernel(x_ref, w_ref, o_ref, *, scale):
    acc = jnp.dot(x_ref[...].astype(BF16), w_ref[...], preferred_element_type=F32)
    if scale != 1.0:
        acc = acc * scale
    o_ref[...] = acc.astype(o_ref.dtype)


def _mm(x, w, out_dtype, rows, scale=1.0, tm=512, tn=1024):
    k = x.shape[1]
    n = w.shape[1]
    row0, m = rows
    tm, tn = min(tm, m), min(tn, n)
    blk0 = row0 // tm
    return pl.pallas_call(
        functools.partial(_mm_kernel, scale=scale),
        grid=(m // tm, n // tn),
        in_specs=[pl.BlockSpec((tm, k), lambda i, j: (i + blk0, 0)),
                  pl.BlockSpec((k, tn), lambda i, j: (0, j))],
        out_specs=pl.BlockSpec((tm, tn), lambda i, j: (i, j)),
        out_shape=jax.ShapeDtypeStruct((m, n), out_dtype),
        compiler_params=_cparams(("parallel", "parallel")),
        name="mm",
    )(x, w)


def _gate_rows_kernel(x_ref, w_ref, b_ref, o_ref, *, kind):
    z = jnp.dot(x_ref[...], w_ref[...], precision=HI, preferred_element_type=F32) + b_ref[...]
    o_ref[...] = _ACT[kind](z)


def _gate_rows(x, w, b, kind, rows, tm=512):
    k = x.shape[1]
    n = w.shape[1]
    row0, m = rows
    blk0 = row0 // tm
    return pl.pallas_call(
        functools.partial(_gate_rows_kernel, kind=kind),
        grid=(m // tm,),
        in_specs=[pl.BlockSpec((tm, k), lambda i: (i + blk0, 0)),
                  pl.BlockSpec((k, n), lambda i: (0, 0)),
                  pl.BlockSpec((1, n), lambda i: (0, 0))],
        out_specs=pl.BlockSpec((tm, n), lambda i: (i, 0)),
        out_shape=jax.ShapeDtypeStruct((m, n), F32),
        compiler_params=_cparams(("parallel",)),
        name="gate_rows",
    )(x, w, b.reshape(1, n))


def _gate_cols_kernel(x_ref, wt_ref, b_ref, o_ref, carry_ref, *, kind, cumsum, out_scale):
    z = _dot_nt(wt_ref[...], x_ref[...], precision=HI) + b_ref[...]
    y = _ACT[kind](z)
    if cumsum:
        @pl.when(pl.program_id(1) == 0)
        def _():
            carry_ref[...] = jnp.zeros_like(carry_ref)
        tl = y.shape[1]
        upper = (_iota((tl, tl), 0) <= _iota((tl, tl), 1)).astype(F32)
        cs = jnp.dot(y, upper, precision=HI, preferred_element_type=F32) + carry_ref[...]
        o_ref[0] = cs * out_scale
        carry_ref[...] = cs[:, tl - 1:tl]
    else:
        o_ref[0] = y * out_scale


def _gate_cols(x, bsz, seq, wt, b, kind, cumsum, out_scale=1.0, tl=512):
    k = x.shape[1]
    n = wt.shape[0]
    tl = min(tl, seq)
    per_seq = seq // tl
    return pl.pallas_call(
        functools.partial(_gate_cols_kernel, kind=kind, cumsum=cumsum, out_scale=out_scale),
        grid=(bsz, per_seq),
        in_specs=[pl.BlockSpec((tl, k), lambda bi, li: (bi * per_seq + li, 0)),
                  pl.BlockSpec((n, k), lambda bi, li: (0, 0)),
                  pl.BlockSpec((n, 1), lambda bi, li: (0, 0))],
        out_specs=pl.BlockSpec((1, n, tl), lambda bi, li: (bi, 0, li)),
        out_shape=jax.ShapeDtypeStruct((bsz, n, seq), F32),
        scratch_shapes=[pltpu.VMEM((n, 1), F32)],
        compiler_params=_cparams(("parallel", "arbitrary")),
        name="gate_cols",
    )(x, wt, b.reshape(n, 1))


def _fox_attn_kernel(q_ref, k_ref, v_ref, c_ref, o_ref, kb_ref, vb_ref, *, tq, tk):
    qi = pl.program_id(2)
    seq = k_ref.shape[1]

    @pl.when(qi == 0)
    def _():
        def cast_block(i, carry):
            rows = pl.ds(pl.multiple_of(i * tq, tq), tq)
            kb_ref[rows, :] = k_ref[0, rows, :].astype(BF16)
            vb_ref[rows, :] = v_ref[0, rows, :].astype(BF16)
            return carry
        lax.fori_loop(0, seq // tq, cast_block, 0)

    q = q_ref[0]
    lane = _iota((tq, LANES), 1)
    first = lane < FOX_HEAD_DIM
    zero = jnp.zeros_like(q)
    qh = (jnp.where(first, q, zero), jnp.where(first, zero, q))

    def step(kb, carry, width, masked):
        rows = pl.ds(pl.multiple_of(kb * width, width), width)
        kblk = kb_ref[rows, :]
        vblk = vb_ref[rows, :]
        if masked:
            visible = (_iota((tq, width), 0) - _iota((tq, width), 1)) >= kb * width - qi * tq
        new = []
        for hh in range(2):
            m, l, acc = carry[hh]
            s = _dot_nt(qh[hh], kblk) - c_ref[0, hh, :, rows]
            if masked:
                s = jnp.where(visible, s, -jnp.inf)
            m_new = jnp.maximum(m, jnp.max(s, axis=1, keepdims=True))
            alpha = jnp.exp2(m - m_new)
            p = jnp.exp2(s - m_new)
            l = alpha * l + jnp.sum(p, axis=1, keepdims=True)
            acc = alpha * acc + jnp.dot(p.astype(BF16), vblk, preferred_element_type=F32)
            new.append((m_new, l, acc))
        return tuple(new)

    init = (jnp.full((tq, 1), -jnp.inf, F32), jnp.zeros((tq, 1), F32), jnp.zeros((tq, LANES), F32))
    n_full = (qi * tq) // tk
    carry = lax.fori_loop(0, n_full, functools.partial(step, width=tk, masked=False), (init, init))
    carry = lax.fori_loop(n_full * (tk // tq), qi, functools.partial(step, width=tq, masked=False), carry)
    carry = step(qi, carry, tq, True)
    outs = [acc / l for _, l, acc in carry]
    o_ref[0] = jnp.where(first, outs[0], outs[1]).astype(o_ref.dtype)


def _fox_attn(q3, k3, v3, c4, tq=512, tk=1024):
    bsz, seq, width = q3.shape
    tq = min(tq, seq)
    tk = min(tk, seq)
    pairs = width // LANES
    return pl.pallas_call(
        functools.partial(_fox_attn_kernel, tq=tq, tk=tk),
        grid=(bsz, pairs, seq // tq),
        in_specs=[pl.BlockSpec((1, tq, LANES), lambda b, j, i: (b, i, j)),
                  pl.BlockSpec((1, seq, LANES), lambda b, j, i: (b, 0, j)),
                  pl.BlockSpec((1, seq, LANES), lambda b, j, i: (b, 0, j)),
                  pl.BlockSpec((1, 2, 1, seq), lambda b, j, i: (b, j, 0, 0))],
        out_specs=pl.BlockSpec((1, tq, LANES), lambda b, j, i: (b, i, j)),
        out_shape=jax.ShapeDtypeStruct((bsz, seq, width), BF16),
        scratch_shapes=[pltpu.VMEM((seq, LANES), BF16), pltpu.VMEM((seq, LANES), BF16)],
        compiler_params=_cparams(("parallel", "parallel", "arbitrary")),
        name="fox_attn",
    )(q3, k3, v3, c4)


def _dec_bias_kernel(*refs, pages_per_step):
    lc_refs = refs[1:1 + pages_per_step]
    ln_ref, cp_ref, cn_ref, carry_ref = refs[1 + pages_per_step:]
    step = pl.program_id(1)

    @pl.when(step == 0)
    def _():
        carry_ref[...] = jnp.zeros_like(carry_ref)

    lower = (_iota((PAGE_SIZE, PAGE_SIZE), 0) >= _iota((PAGE_SIZE, PAGE_SIZE), 1)).astype(F32)
    for i in range(pages_per_step):
        c = jnp.dot(lower, lc_refs[i][0, 0], precision=HI, preferred_element_type=F32) + carry_ref[...]
        cp_ref[0, i] = c * LOG2E
        carry_ref[...] = c[PAGE_SIZE - 1:PAGE_SIZE, :]

    @pl.when(step == pl.num_programs(1) - 1)
    def _():
        c = ln_ref[0]
        ld = c.shape[0]
        row = _iota(c.shape, 0)
        sh = 1
        while sh < ld:
            c = c + jnp.where(row >= sh, pltpu.roll(c, sh, 0), 0.0)
            sh *= 2
        cn_ref[0] = (c + carry_ref[...]) * LOG2E


def _dec_bias(cache_lf, ln3, page_table, layer, pages_per_step):
    bd, ld, _ = ln3.shape
    n_pages = page_table.shape[1]

    def cache_map(i):
        return lambda b, s, pt: (layer, pt[b, s * pages_per_step + i], 0, 0)

    grid_spec = pltpu.PrefetchScalarGridSpec(
        num_scalar_prefetch=1,
        grid=(bd, n_pages // pages_per_step),
        in_specs=[pl.BlockSpec((1, 1, PAGE_SIZE, FOX_HEADS), cache_map(i)) for i in range(pages_per_step)]
        + [pl.BlockSpec((1, ld, FOX_HEADS), lambda b, s, pt: (b, 0, 0))],
        out_specs=[pl.BlockSpec((1, pages_per_step, PAGE_SIZE, FOX_HEADS), lambda b, s, pt: (b, s, 0, 0)),
                   pl.BlockSpec((1, ld, FOX_HEADS), lambda b, s, pt: (b, 0, 0))],
        scratch_shapes=[pltpu.VMEM((1, FOX_HEADS), F32)])
    return pl.pallas_call(
        functools.partial(_dec_bias_kernel, pages_per_step=pages_per_step),
        grid_spec=grid_spec,
        out_shape=[jax.ShapeDtypeStruct((bd, n_pages, PAGE_SIZE, FOX_HEADS), F32),
                   jax.ShapeDtypeStruct((bd, ld, FOX_HEADS), F32)],
        compiler_params=_cparams(("parallel", "arbitrary")),
        name="dec_bias",
    )(page_table, *([cache_lf] * pages_per_step), ln3)


def _fox_dec_kernel(*refs, pages_per_step, ld):
    n = pages_per_step
    q_ref = refs[1]
    kc_refs = refs[2:2 + n]
    vc_refs = refs[2 + n:2 + 2 * n]
    cp_ref, kn_ref, vn_ref, cn_ref, o_ref, qb_ref, mask_ref, m_ref, l_ref, acc_ref = refs[2 + 2 * n:]
    step = pl.program_id(1)
    rows = q_ref.shape[1]

    @pl.when(step == 0)
    def _():
        qb_ref[...] = q_ref[0].astype(BF16)
        head_of_row = _iota((rows, LANES), 0) // ld
        head_of_lane = _iota((rows, LANES), 1) % FOX_HEADS
        mask_ref[...] = jnp.where(head_of_row == head_of_lane, 0.0, -jnp.inf)
        m_ref[...] = jnp.full(m_ref.shape, -jnp.inf, F32)
        l_ref[...] = jnp.zeros_like(l_ref)
        acc_ref[...] = jnp.zeros_like(acc_ref)

    def attend(k2, v2, c_row, causal):
        width = k2.shape[0]
        s = _dot_nt(qb_ref[...], k2.astype(BF16))
        s = (s - c_row) + pltpu.repeat(mask_ref[...], width // LANES, axis=1)
        if causal:
            tok = _iota((rows, width), 0) % ld
            key = _iota((rows, width), 1) // FOX_HEADS
            s = jnp.where(key <= tok, s, -jnp.inf)
        m_old = m_ref[...]
        m_new = jnp.maximum(m_old, jnp.max(s, axis=1, keepdims=True))
        alpha = jnp.exp2(m_old - m_new)
        pr = jnp.exp2(s - m_new)
        l_ref[...] = alpha * l_ref[...] + jnp.sum(pr, axis=1, keepdims=True)
        acc_ref[...] = alpha * acc_ref[...] + jnp.dot(pr.astype(BF16), v2.astype(BF16),
                                                      preferred_element_type=F32)
        m_ref[...] = m_new

    @pl.when(step < pl.num_programs(1) - 1)
    def _():
        for i in range(n):
            k2 = kc_refs[i][0, 0].reshape(PAGE_SIZE * FOX_HEADS, FOX_HEAD_DIM)
            v2 = vc_refs[i][0, 0].reshape(PAGE_SIZE * FOX_HEADS, FOX_HEAD_DIM)
            attend(k2, v2, cp_ref[0, i], False)

    @pl.when(step == pl.num_programs(1) - 1)
    def _():
        attend(kn_ref[0], vn_ref[0], cn_ref[0], True)
        o_ref[0] = acc_ref[...] / l_ref[...]


def _fox_decode(q3, kn3, vn3, ln3, cache_k, cache_v, cache_lf, page_table, layer):
    bd, ld, width = q3.shape
    n_pages = page_table.shape[1]
    rows = FOX_HEADS * ld
    pps = DECODE_PAGES_PER_STEP if n_pages % DECODE_PAGES_PER_STEP == 0 else 1
    n_steps = n_pages // pps
    c_pages, c_new = _dec_bias(cache_lf, ln3, page_table, layer,
                               BIAS_PAGES_PER_STEP if n_pages % BIAS_PAGES_PER_STEP == 0 else 1)
    c_pages = c_pages.reshape(bd, n_pages, 1, PAGE_SIZE * FOX_HEADS)
    c_new = c_new.reshape(bd, 1, rows)
    q2 = q3.reshape(bd, ld, FOX_HEADS, FOX_HEAD_DIM).transpose(0, 2, 1, 3).reshape(bd, rows, FOX_HEAD_DIM)
    k2n = kn3.reshape(bd, rows, FOX_HEAD_DIM)
    v2n = vn3.reshape(bd, rows, FOX_HEAD_DIM)

    def cache_map(i):
        return lambda b, s, pt: (layer, pt[b, jnp.minimum(s, n_steps - 1) * pps + i], 0, 0, 0)

    def seq_map(b, s, pt):
        return (b, 0, 0)

    page_block = (1, 1, PAGE_SIZE, FOX_HEADS, FOX_HEAD_DIM)
    grid_spec = pltpu.PrefetchScalarGridSpec(
        num_scalar_prefetch=1,
        grid=(bd, n_steps + 1),
        in_specs=[pl.BlockSpec((1, rows, FOX_HEAD_DIM), seq_map)]
        + [pl.BlockSpec(page_block, cache_map(i)) for i in range(pps)]
        + [pl.BlockSpec(page_block, cache_map(i)) for i in range(pps)]
        + [pl.BlockSpec((1, pps, 1, PAGE_SIZE * FOX_HEADS),
                        lambda b, s, pt: (b, jnp.minimum(s, n_steps - 1), 0, 0)),
           pl.BlockSpec((1, rows, FOX_HEAD_DIM), seq_map),
           pl.BlockSpec((1, rows, FOX_HEAD_DIM), seq_map),
           pl.BlockSpec((1, 1, rows), seq_map)],
        out_specs=pl.BlockSpec((1, rows, FOX_HEAD_DIM), seq_map),
        scratch_shapes=[pltpu.VMEM((rows, FOX_HEAD_DIM), BF16),
                        pltpu.VMEM((rows, LANES), F32),
                        pltpu.VMEM((rows, 1), F32),
                        pltpu.VMEM((rows, 1), F32),
                        pltpu.VMEM((rows, FOX_HEAD_DIM), F32)])
    o2 = pl.pallas_call(
        functools.partial(_fox_dec_kernel, pages_per_step=pps, ld=ld),
        grid_spec=grid_spec,
        out_shape=jax.ShapeDtypeStruct((bd, rows, FOX_HEAD_DIM), F32),
        compiler_params=_cparams(("parallel", "arbitrary")),
        name="fox_decode",
    )(page_table, q2, *([cache_k] * pps), *([cache_v] * pps), c_pages, k2n, v2n, c_new)
    return o2.reshape(bd, FOX_HEADS, ld, FOX_HEAD_DIM).transpose(0, 2, 1, 3).reshape(bd, ld, width)


def _ssd_chunk_kernel(xbc_ref, z_ref, dt_ref, dtt_ref, cw_ref, cb_ref, al_ref, alt_ref, dsk_ref, nw_ref,
                      y_ref, st_ref, xp_ref, act_ref, ysc_ref, state_ref):
    c = pl.program_id(1)
    nc = pl.num_programs(1)
    q = SSD_CHUNK
    pair_w = 2 * SSD_HEAD_DIM

    @pl.when(c == 0)
    def _():
        state_ref[...] = jnp.zeros_like(state_ref)
        xp_ref[0:SUBLANES, :] = jnp.zeros((SUBLANES, CONV_DIM), F32)

    xp_ref[SUBLANES:SUBLANES + q, :] = xbc_ref[0]
    conv = cb_ref[...]
    for t in range(CONV_K):
        lo = SUBLANES - (CONV_K - 1) + t
        conv = conv + xp_ref[lo:lo + q, :] * cw_ref[t:t + 1, :]
    xp_ref[0:SUBLANES, :] = xp_ref[q:q + SUBLANES, :]
    act_ref[...] = conv * _sigmoid(conv)

    dt = dt_ref[0]
    dtt = dtt_ref[0]
    a = dt * (-jnp.exp(al_ref[...]))
    at = dtt * (-jnp.exp(alt_ref[...]))
    lower = (_iota((q, q), 0) >= _iota((q, q), 1))
    acum = jnp.dot(lower.astype(F32), a, precision=HI, preferred_element_type=F32)
    acum_t = jnp.dot(at, (_iota((q, q), 0) <= _iota((q, q), 1)).astype(F32), precision=HI,
                     preferred_element_type=F32)
    a_end = acum[q - 1:q, :]
    decay_end = jnp.exp(a_end - acum) * dt
    exp_acum = jnp.exp(acum)
    chunk_decay = jnp.exp(acum_t[:, q - 1:q])

    lane = _iota((q, pair_w), 1)
    first = lane < SSD_HEAD_DIM
    row_first = _iota((pair_w, 1), 0) < SSD_HEAD_DIM
    for g in range(SSD_GROUPS):
        bg = act_ref[:, D_INNER + g * D_STATE:D_INNER + (g + 1) * D_STATE].astype(BF16)
        cg = act_ref[:, D_INNER + GN + g * D_STATE:D_INNER + GN + (g + 1) * D_STATE].astype(BF16)
        cb = _dot_nt(cg, bg)
        for pr in range(SSD_HEADS_PER_GROUP // 2):
            pi = g * (SSD_HEADS_PER_GROUP // 2) + pr
            h0 = 2 * pi
            xpair = act_ref[:, pi * pair_w:(pi + 1) * pair_w]
            xpair_b = xpair.astype(BF16)
            ys = []
            for hh in range(2):
                h = h0 + hh
                seg = acum[:, h:h + 1] - acum_t[h:h + 1, :]
                w = cb * jnp.exp(jnp.where(lower, seg, -jnp.inf)) * dtt[h:h + 1, :]
                ys.append(jnp.dot(w.astype(BF16), xpair_b, preferred_element_type=F32))
            y_diag = jnp.where(first, ys[0], ys[1])
            s_pair = state_ref[pi]
            ea = jnp.where(first, exp_acum[:, h0:h0 + 1], exp_acum[:, h0 + 1:h0 + 2])
            y_off = _dot_nt(cg, s_pair.astype(BF16)) * ea
            ysc_ref[:, pi * pair_w:(pi + 1) * pair_w] = y_diag + y_off
            de = jnp.where(first, decay_end[:, h0:h0 + 1], decay_end[:, h0 + 1:h0 + 2])
            contrib = _dot_tn((xpair * de).astype(BF16), bg)
            cd = jnp.where(row_first, chunk_decay[h0:h0 + 1, :], chunk_decay[h0 + 1:h0 + 2, :])
            state_ref[pi] = s_pair * cd + contrib

    gw = D_INNER // SSD_GROUPS
    for g in range(SSD_GROUPS):
        cols = slice(g * gw, (g + 1) * gw)
        zg = z_ref[0, :, cols]
        yg = (ysc_ref[:, cols] + dsk_ref[:, cols] * act_ref[:, cols]) * (zg * _sigmoid(zg))
        ms = jnp.mean(yg * yg, axis=1, keepdims=True)
        y_ref[0, :, cols] = (yg * lax.rsqrt(ms + RMS_EPS) * nw_ref[:, cols]).astype(y_ref.dtype)

    @pl.when(c == nc - 1)
    def _():
        st_ref[0] = state_ref[...]


def _ssd_prompt(xbc3, z3, dt3, dtt3, conv_w, conv_b, a_log, d_lanes, norm_w):
    bsz, seq, _ = xbc3.shape
    q = SSD_CHUNK
    n_pairs = SSD_HEADS // 2
    const2 = lambda b, c: (0, 0)
    y, st = pl.pallas_call(
        _ssd_chunk_kernel,
        grid=(bsz, seq // q),
        in_specs=[pl.BlockSpec((1, q, CONV_DIM), lambda b, c: (b, c, 0)),
                  pl.BlockSpec((1, q, D_INNER), lambda b, c: (b, c, 0)),
                  pl.BlockSpec((1, q, SSD_HEADS), lambda b, c: (b, c, 0)),
                  pl.BlockSpec((1, SSD_HEADS, q), lambda b, c: (b, 0, c)),
                  pl.BlockSpec((CONV_K, CONV_DIM), const2),
                  pl.BlockSpec((1, CONV_DIM), const2),
                  pl.BlockSpec((1, SSD_HEADS), const2),
                  pl.BlockSpec((SSD_HEADS, 1), const2),
                  pl.BlockSpec((1, D_INNER), const2),
                  pl.BlockSpec((1, D_INNER), const2)],
        out_specs=[pl.BlockSpec((1, q, D_INNER), lambda b, c: (b, c, 0)),
                   pl.BlockSpec((1, n_pairs, 2 * SSD_HEAD_DIM, D_STATE), lambda b, c: (b, 0, 0, 0))],
        out_shape=[jax.ShapeDtypeStruct((bsz, seq, D_INNER), BF16),
                   jax.ShapeDtypeStruct((bsz, n_pairs, 2 * SSD_HEAD_DIM, D_STATE), F32)],
        scratch_shapes=[pltpu.VMEM((q + SUBLANES, CONV_DIM), F32),
                        pltpu.VMEM((q, CONV_DIM), F32),
                        pltpu.VMEM((q, D_INNER), F32),
                        pltpu.VMEM((n_pairs, 2 * SSD_HEAD_DIM, D_STATE), F32)],
        compiler_params=_cparams(("parallel", "arbitrary")),
        name="ssd_prompt",
    )(xbc3, z3, dt3, dtt3, conv_w, conv_b.reshape(1, CONV_DIM), a_log.reshape(1, SSD_HEADS),
      a_log.reshape(SSD_HEADS, 1), d_lanes, norm_w.reshape(1, D_INNER))
    return y, st.reshape(bsz, SSD_HEADS, SSD_HEAD_DIM, D_STATE)


def _ssd_step_kernel(xbc_ref, c0_ref, z_ref, dt_ref, s0_ref, cw_ref, cb_ref, al_ref, dsk_ref, nw_ref,
                     hexp_ref, gsum_ref, y_ref, s1_ref, xp_ref, xd_ref, bpad_ref):
    ld = xbc_ref.shape[1]
    tail = CONV_K - 1
    pair_w = 2 * SSD_HEAD_DIM
    gw = D_INNER // SSD_GROUPS

    xp_ref[SUBLANES - tail:SUBLANES, :] = c0_ref[0]
    xp_ref[SUBLANES:SUBLANES + ld, :] = xbc_ref[0]
    conv = cb_ref[...]
    for t in range(CONV_K):
        lo = SUBLANES - tail + t
        conv = conv + xp_ref[lo:lo + ld, :] * cw_ref[t:t + 1, :]
    act = conv * _sigmoid(conv)
    xs = act[:, :D_INNER]
    bm = act[:, D_INNER:D_INNER + GN]
    cm = act[:, D_INNER + GN:]

    dt = dt_ref[0]
    a = dt * (-jnp.exp(al_ref[...]))
    row = _iota((ld, SSD_HEADS), 0)
    acum = a
    sh = 1
    while sh < ld:
        acum = acum + jnp.where(row >= sh, pltpu.roll(acum, sh, 0), 0.0)
        sh *= 2
    a_end = acum[ld - 1:ld, :]
    hexp = hexp_ref[...]

    prod = jnp.concatenate([cm * bm[s:s + 1, :] for s in range(ld)], axis=0)
    cbh = jnp.dot(prod, gsum_ref[...], precision=HI, preferred_element_type=F32)
    a_l = jnp.concatenate([acum] * ld, axis=0)
    a_s = jnp.concatenate([jnp.broadcast_to(acum[s:s + 1, :], (ld, SSD_HEADS)) for s in range(ld)], axis=0)
    dt_s = jnp.concatenate([jnp.broadcast_to(dt[s:s + 1, :], (ld, SSD_HEADS)) for s in range(ld)], axis=0)
    pr_row = _iota((ld * ld, SSD_HEADS), 0)
    causal = (pr_row % ld) >= (pr_row // ld)
    w = cbh * jnp.exp(jnp.where(causal, a_l - a_s, -jnp.inf)) * dt_s
    wexp = jnp.dot(w, hexp, precision=HI, preferred_element_type=F32)
    y = jnp.zeros((ld, D_INNER), F32)
    for s in range(ld):
        y = y + wexp[s * ld:(s + 1) * ld, :] * xs[s:s + 1, :]

    ea = jnp.dot(jnp.exp(acum), hexp, precision=HI, preferred_element_type=F32)
    de = jnp.dot(jnp.exp(a_end - acum) * dt, hexp, precision=HI, preferred_element_type=F32)
    xd_ref[...] = jnp.zeros_like(xd_ref)
    bpad_ref[...] = jnp.zeros_like(bpad_ref)
    xd_ref[0:ld, :] = xs * de
    bpad_ref[0:ld, :] = bm
    chunk_decay = jnp.exp(a_end)
    pairs_per_group = SSD_HEADS_PER_GROUP // 2
    y_off = []
    for g in range(SSD_GROUPS):
        cg = cm[:, g * D_STATE:(g + 1) * D_STATE].astype(BF16)
        bg = bpad_ref[:, g * D_STATE:(g + 1) * D_STATE].astype(BF16)
        for pr in range(pairs_per_group):
            pi = g * pairs_per_group + pr
            s_pair = s0_ref[0, pi]
            y_off.append(_dot_nt(cg, s_pair.astype(BF16)))
            contrib = _dot_tn(xd_ref[:, pi * pair_w:(pi + 1) * pair_w].astype(BF16), bg)
            for hh in range(2):
                h = 2 * pi + hh
                rows = slice(hh * SSD_HEAD_DIM, (hh + 1) * SSD_HEAD_DIM)
                s1_ref[0, pi, rows, :] = s_pair[rows, :] * chunk_decay[0, h] + contrib[rows, :]
    y = y + jnp.concatenate(y_off, axis=1) * ea

    y = (y + dsk_ref[...] * xs) * (z_ref[0] * _sigmoid(z_ref[0]))
    for g in range(SSD_GROUPS):
        cols = slice(g * gw, (g + 1) * gw)
        yg = y[:, cols]
        ms = jnp.mean(yg * yg, axis=1, keepdims=True)
        y_ref[0, :, cols] = yg * lax.rsqrt(ms + RMS_EPS) * nw_ref[:, cols]


def _ssd_sample(xbc3, conv0, z3, dt3, state0, conv_w, conv_b, a_log, d_lanes, norm_w):
    bd, ld, _ = xbc3.shape
    n_pairs = SSD_HEADS // 2
    pair_w = 2 * SSD_HEAD_DIM
    hexp = (jnp.arange(D_INNER, dtype=I32)[None, :] // SSD_HEAD_DIM == jnp.arange(SSD_HEADS, dtype=I32)[:, None]).astype(F32)
    gsum = (jnp.arange(GN, dtype=I32)[:, None] // D_STATE ==
            jnp.arange(SSD_HEADS, dtype=I32)[None, :] // SSD_HEADS_PER_GROUP).astype(F32)
    s0 = state0.reshape(bd, n_pairs, pair_w, D_STATE)
    const2 = lambda b: (0, 0)
    y, s1 = pl.pallas_call(
        _ssd_step_kernel,
        grid=(bd,),
        in_specs=[pl.BlockSpec((1, ld, CONV_DIM), lambda b: (b, 0, 0)),
                  pl.BlockSpec((1, CONV_K - 1, CONV_DIM), lambda b: (b, 0, 0)),
                  pl.BlockSpec((1, ld, D_INNER), lambda b: (b, 0, 0)),
                  pl.BlockSpec((1, ld, SSD_HEADS), lambda b: (b, 0, 0)),
                  pl.BlockSpec((1, n_pairs, pair_w, D_STATE), lambda b: (b, 0, 0, 0)),
                  pl.BlockSpec((CONV_K, CONV_DIM), const2),
                  pl.BlockSpec((1, CONV_DIM), const2),
                  pl.BlockSpec((1, SSD_HEADS), const2),
                  pl.BlockSpec((1, D_INNER), const2),
                  pl.BlockSpec((1, D_INNER), const2),
                  pl.BlockSpec((SSD_HEADS, D_INNER), const2),
                  pl.BlockSpec((GN, SSD_HEADS), const2)],
        out_specs=[pl.BlockSpec((1, ld, D_INNER), lambda b: (b, 0, 0)),
                   pl.BlockSpec((1, n_pairs, pair_w, D_STATE), lambda b: (b, 0, 0, 0))],
        out_shape=[jax.ShapeDtypeStruct((bd, ld, D_INNER), F32),
                   jax.ShapeDtypeStruct((bd, n_pairs, pair_w, D_STATE), F32)],
        scratch_shapes=[pltpu.VMEM((SUBLANES + ld, CONV_DIM), F32),
                        pltpu.VMEM((LANES, D_INNER), F32),
                        pltpu.VMEM((LANES, GN), F32)],
        compiler_params=_cparams(("parallel",)),
        name="ssd_sample",
    )(xbc3, conv0, z3, dt3, s0, conv_w, conv_b.reshape(1, CONV_DIM), a_log.reshape(1, SSD_HEADS),
      d_lanes, norm_w.reshape(1, D_INNER), hexp, gsum)
    return y, s1.reshape(bd, SSD_HEADS, SSD_HEAD_DIM, D_STATE)


def _layer_norm(y, g, b):
    mu = jnp.mean(y, axis=1, keepdims=True)
    d = y - mu
    var = jnp.mean(d * d, axis=1, keepdims=True)
    return d * lax.rsqrt(var + LN_EPS) * g + b


def _post_mix_kernel(o_ref, w_ref, x_ref, g_ref, b_ref, wr_ref, br_ref, x1_ref, te_ref, tg_ref):
    mix = jnp.dot(o_ref[...].astype(BF16), w_ref[...], preferred_element_type=F32)
    x1 = _layer_norm(DEEPNORM_ALPHA * x_ref[...] + mix, g_ref[...], b_ref[...])
    x1_ref[...] = x1
    logits = jnp.dot(x1, wr_ref[...], precision=HI, preferred_element_type=F32) + br_ref[...]
    lane = _iota(logits.shape, 1)
    vals, idxs = [], []
    cur = logits
    for _ in range(TOP_K):
        mx = jnp.max(cur, axis=1, keepdims=True)
        idx = jnp.min(jnp.where(cur == mx, lane, LANES), axis=1, keepdims=True)
        vals.append(mx)
        idxs.append(idx)
        cur = jnp.where(lane == idx, -jnp.inf, cur)
    ex = [jnp.exp(v - vals[0]) for v in vals]
    den = ex[0] + ex[1] + ex[2] + ex[3]
    te = jnp.zeros(logits.shape, I32)
    tg = jnp.zeros(logits.shape, F32)
    for k in range(TOP_K):
        te = jnp.where(lane == k, idxs[k], te)
        tg = jnp.where(lane == k, ex[k] / den, tg)
    te_ref[...] = te
    tg_ref[...] = tg


def _post_mix(o, w_out, x, g, b, w_router, b_router, tm=256):
    m, k = o.shape
    wr = jnp.zeros((D_MODEL, LANES), F32).at[:, :N_EXPERTS].set(w_router)
    br = jnp.full((1, LANES), NEG_BIG, F32).at[0, :N_EXPERTS].set(b_router)
    row = lambda i: (i, 0)
    const = lambda i: (0, 0)
    return pl.pallas_call(
        _post_mix_kernel,
        grid=(m // tm,),
        in_specs=[pl.BlockSpec((tm, k), row),
                  pl.BlockSpec((k, D_MODEL), const),
                  pl.BlockSpec((tm, D_MODEL), row),
                  pl.BlockSpec((1, D_MODEL), const),
                  pl.BlockSpec((1, D_MODEL), const),
                  pl.BlockSpec((D_MODEL, LANES), const),
                  pl.BlockSpec((1, LANES), const)],
        out_specs=[pl.BlockSpec((tm, D_MODEL), row),
                   pl.BlockSpec((tm, LANES), row), pl.BlockSpec((tm, LANES), row)],
        out_shape=[jax.ShapeDtypeStruct((m, D_MODEL), F32),
                   jax.ShapeDtypeStruct((m, LANES), I32), jax.ShapeDtypeStruct((m, LANES), F32)],
        compiler_params=_cparams(("parallel",)),
        name="post_mix",
    )(o, w_out, x, g.reshape(1, D_MODEL), b.reshape(1, D_MODEL), wr, br)


def _rank_kernel(te_ref, rank_ref, cnt_ref, carry_ref):
    @pl.when(pl.program_id(0) == 0)
    def _():
        carry_ref[...] = jnp.zeros_like(carry_ref)
    te = te_ref[...]
    tr = te.shape[0]
    lane = _iota((tr, LANES), 1)
    onehot = jnp.zeros((tr, LANES), F32)
    for k in range(TOP_K):
        onehot = onehot + (lane == te[:, k:k + 1]).astype(F32)
    strict = (_iota((tr, tr), 0) > _iota((tr, tr), 1)).astype(BF16)
    before = jnp.dot(strict, onehot.astype(BF16), preferred_element_type=F32) + carry_ref[0:1, :]
    out = jnp.zeros((tr, LANES), F32)
    for k in range(TOP_K):
        rk = jnp.sum(jnp.where(lane == te[:, k:k + 1], before, 0.0), axis=1, keepdims=True)
        out = jnp.where(lane == k, rk, out)
    rank_ref[...] = out.astype(I32)
    total = carry_ref[0:1, :] + jnp.sum(onehot, axis=0, keepdims=True)
    carry_ref[...] = jnp.broadcast_to(total, carry_ref.shape)
    cnt_ref[...] = jnp.broadcast_to(total, cnt_ref.shape).astype(I32)


def _route_ranks(te, tr=512):
    m = te.shape[0]
    return pl.pallas_call(
        _rank_kernel,
        grid=(m // tr,),
        in_specs=[pl.BlockSpec((tr, LANES), lambda i: (i, 0))],
        out_specs=[pl.BlockSpec((tr, LANES), lambda i: (i, 0)),
                   pl.BlockSpec((SUBLANES, LANES), lambda i: (0, 0))],
        out_shape=[jax.ShapeDtypeStruct((m, LANES), I32), jax.ShapeDtypeStruct((SUBLANES, LANES), I32)],
        scratch_shapes=[pltpu.VMEM((SUBLANES, LANES), F32)],
        compiler_params=_cparams(("arbitrary",)),
        name="route_ranks",
    )(te)


def _expert_kernel(be_ref, nu_ref, x_ref, wgu_ref, bgu_ref, wd_ref, bd_ref, o_ref, wgu_b, wd_b):
    n = pl.program_id(0)
    prev = be_ref[jnp.maximum(n - 1, 0)]
    used = n < nu_ref[0]

    @pl.when(jnp.logical_and(used, jnp.logical_or(n == 0, be_ref[n] != prev)))
    def _():
        wgu_b[...] = wgu_ref[0, 0].astype(BF16)
        wd_b[...] = wd_ref[0, 0].astype(BF16)

    @pl.when(used)
    def _():
        h = jnp.dot(x_ref[...].astype(BF16), wgu_b[...], preferred_element_type=F32) + bgu_ref[0, 0]
        g = jnp.minimum(h[:, :D_FF], SWIGLU_LIMIT)
        u = jnp.clip(h[:, D_FF:], -SWIGLU_LIMIT, SWIGLU_LIMIT)
        act = g * _sigmoid(GLU_ALPHA * g) * (u + 1.0)
        o_ref[...] = jnp.dot(act.astype(BF16), wd_b[...], preferred_element_type=F32) + bd_ref[0, 0]

    @pl.when(jnp.logical_not(used))
    def _():
        o_ref[...] = jnp.zeros_like(o_ref)


def _experts(xb, block_e, n_used, w_gu, b_gu, w_down, b_down, layer):
    rows = xb.shape[0]
    nb = rows // MOE_ROWS

    def wmap(n, be, nu):
        return (layer, be[n], 0, 0)

    grid_spec = pltpu.PrefetchScalarGridSpec(
        num_scalar_prefetch=2,
        grid=(nb,),
        in_specs=[pl.BlockSpec((MOE_ROWS, D_MODEL), lambda n, be, nu: (n, 0)),
                  pl.BlockSpec((1, 1, D_MODEL, 2 * D_FF), wmap),
                  pl.BlockSpec((1, 1, 1, 2 * D_FF), wmap),
                  pl.BlockSpec((1, 1, D_FF, D_MODEL), wmap),
                  pl.BlockSpec((1, 1, 1, D_MODEL), wmap)],
        out_specs=pl.BlockSpec((MOE_ROWS, D_MODEL), lambda n, be, nu: (n, 0)),
        scratch_shapes=[pltpu.VMEM((D_MODEL, 2 * D_FF), BF16), pltpu.VMEM((D_FF, D_MODEL), BF16)])
    return pl.pallas_call(
        _expert_kernel,
        grid_spec=grid_spec,
        out_shape=jax.ShapeDtypeStruct((rows, D_MODEL), F32),
        compiler_params=_cparams(("arbitrary",)),
        name="experts",
    )(block_e, n_used, xb, w_gu, b_gu.reshape(DEPTH, N_EXPERTS, 1, 2 * D_FF), w_down,
      b_down.reshape(DEPTH, N_EXPERTS, 1, D_MODEL))


def _finish_kernel(x1_ref, yk_ref, tg_ref, g_ref, b_ref, wg_ref, p_ref, wp_ref, o_ref):
    tg = tg_ref[...]
    moe = jnp.zeros(x1_ref.shape, F32)
    for k in range(TOP_K):
        moe = moe + tg[:, k:k + 1] * yk_ref[k]
    x2 = _layer_norm(DEEPNORM_ALPHA * x1_ref[...] + moe, g_ref[...], b_ref[...])
    gate = _sigmoid(jnp.dot(x2.astype(BF16), wg_ref[...], preferred_element_type=F32))
    proj = jnp.dot(p_ref[...].astype(BF16), wp_ref[...], preferred_element_type=F32)
    o_ref[...] = x2 + gate * proj


def _finish(x1, yk, tg, g, b, w_pg, p, w_pp, tm=256):
    m = x1.shape[0]
    row = lambda i: (i, 0)
    const = lambda i: (0, 0)
    return pl.pallas_call(
        _finish_kernel,
        grid=(m // tm,),
        in_specs=[pl.BlockSpec((tm, D_MODEL), row),
                  pl.BlockSpec((TOP_K, tm, D_MODEL), lambda i: (0, i, 0)),
                  pl.BlockSpec((tm, LANES), row),
                  pl.BlockSpec((1, D_MODEL), const),
                  pl.BlockSpec((1, D_MODEL), const),
                  pl.BlockSpec((D_MODEL, D_MODEL), const),
                  pl.BlockSpec((tm, PLE_DIM), row),
                  pl.BlockSpec((PLE_DIM, D_MODEL), const)],
        out_specs=pl.BlockSpec((tm, D_MODEL), row),
        out_shape=jax.ShapeDtypeStruct((m, D_MODEL), F32),
        compiler_params=_cparams(("parallel",)),
        name="finish",
    )(x1, yk, tg, g.reshape(1, D_MODEL), b.reshape(1, D_MODEL), w_pg, p, w_pp)


def _moe_and_finish(layer, x1, te, tg, p_all, ln2_g, ln2_b, w_gu, b_gu, w_down, b_down, w_pg, w_pp):
    t = x1.shape[0]
    a = t * TOP_K
    rank, cnt = _route_ranks(te)
    counts = cnt[0, :N_EXPERTS]
    padded = (counts + MOE_ROWS - 1) // MOE_ROWS * MOE_ROWS
    pend = jnp.cumsum(padded)
    pstart = pend - padded
    e4 = te[:, :TOP_K]
    dest = pstart[e4] + rank[:, :TOP_K]
    nb = a // MOE_ROWS + N_EXPERTS
    tok = jnp.broadcast_to(jnp.arange(t, dtype=I32)[:, None], (t, TOP_K))
    slot_tok = jnp.zeros((nb * MOE_ROWS,), I32).at[dest.reshape(-1)].set(tok.reshape(-1))
    block_start = jnp.arange(nb, dtype=I32) * MOE_ROWS
    block_e = jnp.minimum(jnp.sum((pend[None, :] <= block_start[:, None]).astype(I32), axis=1), N_EXPERTS - 1)
    n_used = (pend[-1:] // MOE_ROWS).astype(I32)
    xb = x1[slot_tok]
    yb = _experts(xb, block_e, n_used, w_gu, b_gu, w_down, b_down, layer)
    yk = yb[dest.T]
    return _finish(x1, yk, tg, ln2_g, ln2_b, w_pg, p_all, w_pp)


def kernel(x_prompt, x_sample, cache_k, cache_v, cache_logf, state_ssm, state_conv, page_table, p_prompt, p_sample,
           fox_w_in, fox_b_f, fox_w_out, ssd_w_in, ssd_conv_w, ssd_conv_b, ssd_dt_bias, ssd_a_log, ssd_d,
           ssd_norm_w, ssd_w_out, ln1_g, ln1_b, ln2_g, ln2_b, moe_w_router, moe_b_router, moe_w_gu, moe_b_gu,
           moe_w_down, moe_b_down, ple_w_gate, ple_w_proj):
    bsz, seq, _ = x_prompt.shape
    bd, ld, _ = x_sample.shape
    tp = bsz * seq
    ts = bd * ld
    x = jnp.concatenate([x_prompt.reshape(tp, D_MODEL), x_sample.reshape(ts, D_MODEL)], axis=0)
    p_all = jnp.concatenate([p_prompt.reshape(DEPTH, tp, PLE_DIM), p_sample.reshape(DEPTH, ts, PLE_DIM)], axis=1)

    prompt, sample = (0, tp), (tp, ts)
    kp, vp, lfp, hp, cp = [], [], [], [], []
    ks_, vs_, lfs, hs, cs = [], [], [], [], []
    for i in range(DEPTH):
        j = i // N_MIXERS
        if i % N_MIXERS == 0:
            w_in = fox_w_in[j]
            wq = w_in[:, :FOX_WIDTH].astype(BF16)
            wk = w_in[:, FOX_WIDTH:2 * FOX_WIDTH].astype(BF16)
            wv = w_in[:, 2 * FOX_WIDTH:3 * FOX_WIDTH].astype(BF16)
            wf = w_in[:, 3 * FOX_WIDTH:]
            k_p3 = _mm(x, wk, F32, prompt).reshape(bsz, seq, FOX_WIDTH)
            v_p3 = _mm(x, wv, F32, prompt).reshape(bsz, seq, FOX_WIDTH)
            k_s3 = _mm(x, wk, F32, sample).reshape(bd, ld, FOX_WIDTH)
            v_s3 = _mm(x, wv, F32, sample).reshape(bd, ld, FOX_WIDTH)
            lf_p3 = _gate_rows(x, wf, fox_b_f[j], "log_sigmoid", prompt).reshape(bsz, seq, FOX_HEADS)
            lf_s3 = _gate_rows(x, wf, fox_b_f[j], "log_sigmoid", sample).reshape(bd, ld, FOX_HEADS)
            q_p = _mm(x, wq, BF16, prompt, scale=FOX_SCALE * LOG2E).reshape(bsz, seq, FOX_WIDTH)
            q_s = _mm(x, wq, F32, sample, scale=FOX_SCALE * LOG2E).reshape(bd, ld, FOX_WIDTH)
            c4 = _gate_cols(x, bsz, seq, wf.T, fox_b_f[j], "log_sigmoid", True, LOG2E)
            c4 = c4.reshape(bsz, FOX_HEADS, 1, seq)
            o_p = _fox_attn(q_p, k_p3, v_p3, c4)
            o_s = _fox_decode(q_s, k_s3, v_s3, lf_s3, cache_k, cache_v, cache_logf, page_table, j)
            o_all = jnp.concatenate([o_p.reshape(tp, FOX_WIDTH), o_s.reshape(ts, FOX_WIDTH).astype(BF16)], axis=0)
            w_out = fox_w_out[j].astype(BF16)
            kp.append(k_p3.reshape(bsz, seq, FOX_HEADS, FOX_HEAD_DIM))
            vp.append(v_p3.reshape(bsz, seq, FOX_HEADS, FOX_HEAD_DIM))
            lfp.append(lf_p3)
            ks_.append(k_s3.reshape(bd, ld, FOX_HEADS, FOX_HEAD_DIM))
            vs_.append(v_s3.reshape(bd, ld, FOX_HEADS, FOX_HEAD_DIM))
            lfs.append(lf_s3)
        else:
            w_in = ssd_w_in[j]
            wz = w_in[:, :D_INNER].astype(BF16)
            wx = w_in[:, D_INNER:D_INNER + CONV_DIM].astype(BF16)
            wd = w_in[:, D_INNER + CONV_DIM:]
            z_p3 = _mm(x, wz, F32, prompt).reshape(bsz, seq, D_INNER)
            z_s3 = _mm(x, wz, F32, sample).reshape(bd, ld, D_INNER)
            xbc_p3 = _mm(x, wx, F32, prompt).reshape(bsz, seq, CONV_DIM)
            xbc_s3 = _mm(x, wx, F32, sample).reshape(bd, ld, CONV_DIM)
            dt_p3 = _gate_rows(x, wd, ssd_dt_bias[j], "softplus", prompt).reshape(bsz, seq, SSD_HEADS)
            dt_s3 = _gate_rows(x, wd, ssd_dt_bias[j], "softplus", sample).reshape(bd, ld, SSD_HEADS)
            dtt = _gate_cols(x, bsz, seq, wd.T, ssd_dt_bias[j], "softplus", False)
            d_lanes = jnp.repeat(ssd_d[j], SSD_HEAD_DIM).reshape(1, D_INNER)
            y_p, h_p = _ssd_prompt(xbc_p3, z_p3, dt_p3, dtt,
                                   ssd_conv_w[j], ssd_conv_b[j], ssd_a_log[j], d_lanes, ssd_norm_w[j])
            y_s, h_s = _ssd_sample(xbc_s3, state_conv[j], z_s3, dt_s3, state_ssm[j],
                                   ssd_conv_w[j], ssd_conv_b[j], ssd_a_log[j], d_lanes, ssd_norm_w[j])
            o_all = jnp.concatenate([y_p.reshape(tp, D_INNER), y_s.reshape(ts, D_INNER).astype(BF16)], axis=0)
            w_out = ssd_w_out[j].astype(BF16)
            tail = CONV_K - 1
            hp.append(h_p)
            cp.append(xbc_p3[:, seq - tail:, :])
            hs.append(h_s)
            cs.append(jnp.concatenate([state_conv[j], xbc_s3], axis=1)[:, ld:, :])
        x1, te, tg = _post_mix(o_all, w_out, x, ln1_g[i], ln1_b[i], moe_w_router[i], moe_b_router[i])
        x = _moe_and_finish(i, x1, te, tg, p_all[i], ln2_g[i], ln2_b[i], moe_w_gu, moe_b_gu,
                            moe_w_down, moe_b_down, ple_w_gate[i].astype(BF16), ple_w_proj[i].astype(BF16))
    y_prompt = x[:tp].reshape(bsz, seq, D_MODEL)
    y_sample = x[tp:].reshape(bd, ld, D_MODEL)
    return (y_prompt, y_sample, jnp.stack(kp), jnp.stack(vp), jnp.stack(lfp), jnp.stack(hp), jnp.stack(cp),
            jnp.stack(ks_), jnp.stack(vs_), jnp.stack(lfs), jnp.stack(hs), jnp.stack(cs))
```

```python
import functools

import jax
import jax.numpy as jnp
from jax import lax
from jax.experimental import pallas as pl
from jax.experimental.pallas import tpu as pltpu

F32 = jnp.float32
BF16 = jnp.bfloat16
I32 = jnp.int32
HI = lax.Precision.HIGHEST

D_MODEL = 1024
DEPTH = 4
PAGE_SIZE = 128
N_MIXERS = 2
FOX_HEAD_DIM = 64
FOX_HEADS = D_MODEL // FOX_HEAD_DIM
FOX_WIDTH = FOX_HEADS * FOX_HEAD_DIM
FOX_SCALE = FOX_HEAD_DIM ** -0.5
D_INNER = 2 * D_MODEL
SSD_HEAD_DIM = 64
SSD_HEADS = D_INNER // SSD_HEAD_DIM
SSD_GROUPS = 8
SSD_HEADS_PER_GROUP = SSD_HEADS // SSD_GROUPS
D_STATE = 128
CONV_K = 4
GN = SSD_GROUPS * D_STATE
CONV_DIM = D_INNER + 2 * GN
SSD_CHUNK = 128
RMS_EPS = 1e-5
N_EXPERTS = 32
TOP_K = 4
D_FF = D_MODEL
SWIGLU_LIMIT = 7.0
GLU_ALPHA = 1.702
PLE_DIM = 256
LN_EPS = 1e-5
DEEPNORM_ALPHA = (2 * DEPTH) ** 0.25

LANES = 128
SUBLANES = 8
MOE_ROWS = 256
VMEM_LIMIT = 56 * 1024 * 1024
NEG_BIG = -1e30
LOG2E = 1.4426950408889634
DECODE_PAGES_PER_STEP = 8


def _cparams(sem):
    return pltpu.CompilerParams(dimension_semantics=sem, vmem_limit_bytes=VMEM_LIMIT)


def _iota(shape, dim):
    return lax.broadcasted_iota(I32, shape, dim)


def _log_sigmoid(z):
    return jnp.minimum(z, 0.0) - jnp.log1p(jnp.exp(-jnp.abs(z)))


def _softplus(z):
    return jnp.maximum(z, 0.0) + jnp.log1p(jnp.exp(-jnp.abs(z)))


def _sigmoid(z):
    return 1.0 / (1.0 + jnp.exp(-z))


_ACT = {"log_sigmoid": _log_sigmoid, "softplus": _softplus}


def _dot_nt(a, b, **kw):
    return lax.dot_general(a, b, (((1,), (1,)), ((), ())), preferred_element_type=F32, **kw)


def _dot_tn(a, b, **kw):
    return lax.dot_general(a, b, (((0,), (0,)), ((), ())), preferred_element_type=F32, **kw)


def _mm_kernel(x_ref, w_ref, o_ref, *, scale):
    acc = jnp.dot(x_ref[...].astype(BF16), w_ref[...], preferred_element_type=F32)
    if scale != 1.0:
        acc = acc * scale
    o_ref[...] = acc.astype(o_ref.dtype)


def _mm(x, w, out_dtype, rows, scale=1.0, tm=512, tn=1024):
    k = x.shape[1]
    n = w.shape[1]
    row0, m = rows
    tm, tn = min(tm, m), min(tn, n)
    blk0 = row0 // tm
    return pl.pallas_call(
        functools.partial(_mm_kernel, scale=scale),
        grid=(m // tm, n // tn),
        in_specs=[pl.BlockSpec((tm, k), lambda i, j: (i + blk0, 0)),
                  pl.BlockSpec((k, tn), lambda i, j: (0, j))],
        out_specs=pl.BlockSpec((tm, tn), lambda i, j: (i, j)),
        out_shape=jax.ShapeDtypeStruct((m, n), out_dtype),
        compiler_params=_cparams(("parallel", "parallel")),
        name="mm",
    )(x, w)


def _gate_rows_kernel(x_ref, w_ref, b_ref, o_ref, *, kind):
    z = jnp.dot(x_ref[...], w_ref[...], precision=HI, preferred_element_type=F32) + b_ref[...]
    o_ref[...] = _ACT[kind](z)


def _gate_rows(x, w, b, kind, rows, tm=512):
    k = x.shape[1]
    n = w.shape[1]
    row0, m = rows
    blk0 = row0 // tm
    return pl.pallas_call(
        functools.partial(_gate_rows_kernel, kind=kind),
        grid=(m // tm,),
        in_specs=[pl.BlockSpec((tm, k), lambda i: (i + blk0, 0)),
                  pl.BlockSpec((k, n), lambda i: (0, 0)),
                  pl.BlockSpec((1, n), lambda i: (0, 0))],
        out_specs=pl.BlockSpec((tm, n), lambda i: (i, 0)),
        out_shape=jax.ShapeDtypeStruct((m, n), F32),
        compiler_params=_cparams(("parallel",)),
        name="gate_rows",
    )(x, w, b.reshape(1, n))


def _gate_cols_kernel(x_ref, wt_ref, b_ref, o_ref, carry_ref, *, kind, cumsum, out_scale):
    z = _dot_nt(wt_ref[...], x_ref[...], precision=HI) + b_ref[...]
    y = _ACT[kind](z)
    if cumsum:
        @pl.when(pl.program_id(1) == 0)
        def _():
            carry_ref[...] = jnp.zeros_like(carry_ref)
        tl = y.shape[1]
        upper = (_iota((tl, tl), 0) <= _iota((tl, tl), 1)).astype(F32)
        cs = jnp.dot(y, upper, precision=HI, preferred_element_type=F32) + carry_ref[...]
        o_ref[0] = cs * out_scale
        carry_ref[...] = cs[:, tl - 1:tl]
    else:
        o_ref[0] = y * out_scale


def _gate_cols(x, bsz, seq, wt, b, kind, cumsum, out_scale=1.0, tl=512):
    k = x.shape[1]
    n = wt.shape[0]
    tl = min(tl, seq)
    per_seq = seq // tl
    return pl.pallas_call(
        functools.partial(_gate_cols_kernel, kind=kind, cumsum=cumsum, out_scale=out_scale),
        grid=(bsz, per_seq),
        in_specs=[pl.BlockSpec((tl, k), lambda bi, li: (bi * per_seq + li, 0)),
                  pl.BlockSpec((n, k), lambda bi, li: (0, 0)),
                  pl.BlockSpec((n, 1), lambda bi, li: (0, 0))],
        out_specs=pl.BlockSpec((1, n, tl), lambda bi, li: (bi, 0, li)),
        out_shape=jax.ShapeDtypeStruct((bsz, n, seq), F32),
        scratch_shapes=[pltpu.VMEM((n, 1), F32)],
        compiler_params=_cparams(("parallel", "arbitrary")),
        name="gate_cols",
    )(x, wt, b.reshape(n, 1))


def _fox_attn_kernel(q_ref, k_ref, v_ref, c_ref, o_ref, kb_ref, vb_ref, *, tq, tk):
    qi = pl.program_id(2)
    seq = k_ref.shape[1]

    @pl.when(qi == 0)
    def _():
        def cast_block(i, carry):
            rows = pl.ds(pl.multiple_of(i * tq, tq), tq)
            kb_ref[rows, :] = k_ref[0, rows, :].astype(BF16)
            vb_ref[rows, :] = v_ref[0, rows, :].astype(BF16)
            return carry
        lax.fori_loop(0, seq // tq, cast_block, 0)

    q = q_ref[0]
    lane = _iota((tq, LANES), 1)
    first = lane < FOX_HEAD_DIM
    zero = jnp.zeros_like(q)
    qh = (jnp.where(first, q, zero), jnp.where(first, zero, q))

    def step(kb, carry, width, masked):
        rows = pl.ds(pl.multiple_of(kb * width, width), width)
        kblk = kb_ref[rows, :]
        vblk = vb_ref[rows, :]
        if masked:
            visible = (_iota((tq, width), 0) - _iota((tq, width), 1)) >= kb * width - qi * tq
        new = []
        for hh in range(2):
            m, l, acc = carry[hh]
            s = _dot_nt(qh[hh], kblk) - c_ref[0, hh, :, rows]
            if masked:
                s = jnp.where(visible, s, -jnp.inf)
            m_new = jnp.maximum(m, jnp.max(s, axis=1, keepdims=True))
            alpha = jnp.exp2(m - m_new)
            p = jnp.exp2(s - m_new)
            l = alpha * l + jnp.sum(p, axis=1, keepdims=True)
            acc = alpha * acc + jnp.dot(p.astype(BF16), vblk, preferred_element_type=F32)
            new.append((m_new, l, acc))
        return tuple(new)

    init = (jnp.full((tq, 1), -jnp.inf, F32), jnp.zeros((tq, 1), F32), jnp.zeros((tq, LANES), F32))
    n_full = (qi * tq) // tk
    carry = lax.fori_loop(0, n_full, functools.partial(step, width=tk, masked=False), (init, init))
    carry = lax.fori_loop(n_full * (tk // tq), qi, functools.partial(step, width=tq, masked=False), carry)
    carry = step(qi, carry, tq, True)
    outs = [acc / l for _, l, acc in carry]
    o_ref[0] = jnp.where(first, outs[0], outs[1]).astype(o_ref.dtype)


def _fox_attn(q3, k3, v3, c4, tq=512, tk=1024):
    bsz, seq, width = q3.shape
    tq = min(tq, seq)
    tk = min(tk, seq)
    pairs = width // LANES
    return pl.pallas_call(
        functools.partial(_fox_attn_kernel, tq=tq, tk=tk),
        grid=(bsz, pairs, seq // tq),
        in_specs=[pl.BlockSpec((1, tq, LANES), lambda b, j, i: (b, i, j)),
                  pl.BlockSpec((1, seq, LANES), lambda b, j, i: (b, 0, j)),
                  pl.BlockSpec((1, seq, LANES), lambda b, j, i: (b, 0, j)),
                  pl.BlockSpec((1, 2, 1, seq), lambda b, j, i: (b, j, 0, 0))],
        out_specs=pl.BlockSpec((1, tq, LANES), lambda b, j, i: (b, i, j)),
        out_shape=jax.ShapeDtypeStruct((bsz, seq, width), BF16),
        scratch_shapes=[pltpu.VMEM((seq, LANES), BF16), pltpu.VMEM((seq, LANES), BF16)],
        compiler_params=_cparams(("parallel", "parallel", "arbitrary")),
        name="fox_attn",
    )(q3, k3, v3, c4)


def _fox_dec_kernel(*refs, pages_per_step):
    n = pages_per_step
    q_ref = refs[1]
    kc_refs = refs[2:2 + n]
    vc_refs = refs[2 + n:2 + 2 * n]
    lc_refs = refs[2 + 2 * n:2 + 3 * n]
    (kn_ref, vn_ref, ln_ref, o_ref,
     qbd_ref, kpad_ref, vpad_ref, m_ref, l_ref, acc_ref, carry_ref) = refs[2 + 3 * n:]
    step = pl.program_id(1)
    ld = q_ref.shape[1]
    rows = FOX_HEADS * ld
    width = q_ref.shape[2]

    @pl.when(step == 0)
    def _():
        q = q_ref[0]
        qt = jnp.broadcast_to(q[None], (FOX_HEADS, ld, width)).reshape(rows, width)
        head_of_row = _iota((rows, width), 0) // ld
        head_of_col = _iota((rows, width), 1) // FOX_HEAD_DIM
        qbd_ref[...] = jnp.where(head_of_row == head_of_col, qt, 0.0).astype(BF16)
        m_ref[...] = jnp.full(m_ref.shape, -jnp.inf, F32)
        l_ref[...] = jnp.zeros_like(l_ref)
        acc_ref[...] = jnp.zeros_like(acc_ref)
        carry_ref[...] = jnp.zeros_like(carry_ref)

    def attend(scores, lf_ts, pvs, causal):
        upper = (_iota((PAGE_SIZE, PAGE_SIZE), 0) <= _iota((PAGE_SIZE, PAGE_SIZE), 1)).astype(F32)
        offset = carry_ref[...]
        biased = []
        for s, lf_t in zip(scores, lf_ts):
            c = jnp.dot(lf_t, upper, precision=HI, preferred_element_type=F32)
            c_rows = jnp.broadcast_to(((c + offset) * LOG2E)[:, None, :], (FOX_HEADS, ld, PAGE_SIZE))
            biased.append(s - c_rows.reshape(rows, PAGE_SIZE))
            offset = offset + c[:, PAGE_SIZE - 1:PAGE_SIZE]
        carry_ref[...] = offset
        s = jnp.concatenate(biased, axis=1) if len(biased) > 1 else biased[0]
        if causal:
            tok = _iota(s.shape, 0) % ld
            key = _iota(s.shape, 1)
            s = jnp.where(key <= tok, s, -jnp.inf)
        m_old = m_ref[...]
        m_new = jnp.maximum(m_old, jnp.max(s, axis=1, keepdims=True))
        alpha = jnp.exp2(m_old - m_new)
        pr = jnp.exp2(s - m_new)
        l_ref[...] = alpha * l_ref[...] + jnp.sum(pr, axis=1, keepdims=True)
        pb = pr.astype(BF16)
        pv = pvs[0](pb[:, 0:PAGE_SIZE])
        for i in range(1, len(pvs)):
            pv = pv + pvs[i](pb[:, i * PAGE_SIZE:(i + 1) * PAGE_SIZE])
        acc_ref[...] = alpha * acc_ref[...] + pv
        m_ref[...] = m_new

    @pl.when(step < pl.num_programs(1) - 1)
    def _():
        q = qbd_ref[...]
        scores = [jnp.dot(q, kc_refs[i][0, 0].astype(BF16), preferred_element_type=F32) for i in range(n)]
        pvs = [lambda p, i=i: _dot_nt(p, vc_refs[i][0, 0].astype(BF16)) for i in range(n)]
        attend(scores, [lc_refs[i][0, 0] for i in range(n)], pvs, False)

    @pl.when(step == pl.num_programs(1) - 1)
    def _():
        kpad_ref[...] = jnp.zeros_like(kpad_ref)
        vpad_ref[...] = jnp.zeros_like(vpad_ref)
        kpad_ref[0:ld, :] = kn_ref[0]
        vpad_ref[0:ld, :] = vn_ref[0]
        s = _dot_nt(qbd_ref[...], kpad_ref[...].astype(BF16))
        attend([s], [ln_ref[0]], [lambda p: jnp.dot(p, vpad_ref[...].astype(BF16), preferred_element_type=F32)], True)
        a3 = (acc_ref[...] / l_ref[...]).reshape(FOX_HEADS, ld, width)
        own = _iota((FOX_HEADS, ld, width), 0) == _iota((FOX_HEADS, ld, width), 2) // FOX_HEAD_DIM
        o_ref[0] = jnp.sum(jnp.where(own, a3, 0.0), axis=0)


def _fox_decode(q3, kn3, vn3, ln3, cache_k, cache_v, cache_lf, page_table, layer):
    bd, ld, width = q3.shape
    n_fox, n_pool = cache_k.shape[:2]
    n_pages = page_table.shape[1]
    rows = FOX_HEADS * ld
    pps = DECODE_PAGES_PER_STEP if n_pages % DECODE_PAGES_PER_STEP == 0 else 1
    n_steps = n_pages // pps
    kc_t = cache_k.transpose(0, 1, 3, 4, 2).reshape(n_fox, n_pool, width, PAGE_SIZE)
    vc_t = cache_v.transpose(0, 1, 3, 4, 2).reshape(n_fox, n_pool, width, PAGE_SIZE)
    lc_t = cache_lf.transpose(0, 1, 3, 2)
    ln_t = jnp.zeros((bd, FOX_HEADS, PAGE_SIZE), F32).at[:, :, :ld].set(ln3.transpose(0, 2, 1))

    def cache_map(i):
        return lambda b, s, pt: (layer, pt[b, jnp.minimum(s, n_steps - 1) * pps + i], 0, 0)

    def seq_map(b, s, pt):
        return (b, 0, 0)

    grid_spec = pltpu.PrefetchScalarGridSpec(
        num_scalar_prefetch=1,
        grid=(bd, n_steps + 1),
        in_specs=[pl.BlockSpec((1, ld, width), seq_map)]
        + [pl.BlockSpec((1, 1, width, PAGE_SIZE), cache_map(i)) for i in range(pps)]
        + [pl.BlockSpec((1, 1, width, PAGE_SIZE), cache_map(i)) for i in range(pps)]
        + [pl.BlockSpec((1, 1, FOX_HEADS, PAGE_SIZE), cache_map(i)) for i in range(pps)]
        + [pl.BlockSpec((1, ld, width), seq_map),
           pl.BlockSpec((1, ld, width), seq_map),
           pl.BlockSpec((1, FOX_HEADS, PAGE_SIZE), seq_map)],
        out_specs=pl.BlockSpec((1, ld, width), seq_map),
        scratch_shapes=[pltpu.VMEM((rows, width), BF16),
                        pltpu.VMEM((PAGE_SIZE, width), F32),
                        pltpu.VMEM((PAGE_SIZE, width), F32),
                        pltpu.VMEM((rows, 1), F32),
                        pltpu.VMEM((rows, 1), F32),
                        pltpu.VMEM((rows, width), F32),
                        pltpu.VMEM((FOX_HEADS, 1), F32)])
    return pl.pallas_call(
        functools.partial(_fox_dec_kernel, pages_per_step=pps),
        grid_spec=grid_spec,
        out_shape=jax.ShapeDtypeStruct((bd, ld, width), F32),
        compiler_params=_cparams(("parallel", "arbitrary")),
        name="fox_decode",
    )(page_table, q3, *([kc_t] * pps), *([vc_t] * pps), *([lc_t] * pps), kn3, vn3, ln_t)


def _ssd_chunk_kernel(xbc_ref, z_ref, dt_ref, dtt_ref, cw_ref, cb_ref, al_ref, alt_ref, dsk_ref, nw_ref,
                      y_ref, st_ref, xp_ref, act_ref, ysc_ref, state_ref):
    c = pl.program_id(1)
    nc = pl.num_programs(1)
    q = SSD_CHUNK
    pair_w = 2 * SSD_HEAD_DIM

    @pl.when(c == 0)
    def _():
        state_ref[...] = jnp.zeros_like(state_ref)
        xp_ref[0:SUBLANES, :] = jnp.zeros((SUBLANES, CONV_DIM), F32)

    xp_ref[SUBLANES:SUBLANES + q, :] = xbc_ref[0]
    conv = cb_ref[...]
    for t in range(CONV_K):
        lo = SUBLANES - (CONV_K - 1) + t
        conv = conv + xp_ref[lo:lo + q, :] * cw_ref[t:t + 1, :]
    xp_ref[0:SUBLANES, :] = xp_ref[q:q + SUBLANES, :]
    act_ref[...] = conv * _sigmoid(conv)

    dt = dt_ref[0]
    dtt = dtt_ref[0]
    a = dt * (-jnp.exp(al_ref[...]))
    at = dtt * (-jnp.exp(alt_ref[...]))
    lower = (_iota((q, q), 0) >= _iota((q, q), 1))
    acum = jnp.dot(lower.astype(F32), a, precision=HI, preferred_element_type=F32)
    acum_t = jnp.dot(at, (_iota((q, q), 0) <= _iota((q, q), 1)).astype(F32), precision=HI,
                     preferred_element_type=F32)
    a_end = acum[q - 1:q, :]
    decay_end = jnp.exp(a_end - acum) * dt
    exp_acum = jnp.exp(acum)
    chunk_decay = jnp.exp(acum_t[:, q - 1:q])

    lane = _iota((q, pair_w), 1)
    first = lane < SSD_HEAD_DIM
    row_first = _iota((pair_w, 1), 0) < SSD_HEAD_DIM
    for g in range(SSD_GROUPS):
        bg = act_ref[:, D_INNER + g * D_STATE:D_INNER + (g + 1) * D_STATE].astype(BF16)
        cg = act_ref[:, D_INNER + GN + g * D_STATE:D_INNER + GN + (g + 1) * D_STATE].astype(BF16)
        cb = _dot_nt(cg, bg)
        for pr in range(SSD_HEADS_PER_GROUP // 2):
            pi = g * (SSD_HEADS_PER_GROUP // 2) + pr
            h0 = 2 * pi
            xpair = act_ref[:, pi * pair_w:(pi + 1) * pair_w]
            xpair_b = xpair.astype(BF16)
            ys = []
            for hh in range(2):
                h = h0 + hh
                seg = acum[:, h:h + 1] - acum_t[h:h + 1, :]
                w = cb * jnp.exp(jnp.where(lower, seg, -jnp.inf)) * dtt[h:h + 1, :]
                ys.append(jnp.dot(w.astype(BF16), xpair_b, preferred_element_type=F32))
            y_diag = jnp.where(first, ys[0], ys[1])
            s_pair = state_ref[pi]
            ea = jnp.where(first, exp_acum[:, h0:h0 + 1], exp_acum[:, h0 + 1:h0 + 2])
            y_off = _dot_nt(cg, s_pair.astype(BF16)) * ea
            ysc_ref[:, pi * pair_w:(pi + 1) * pair_w] = y_diag + y_off
            de = jnp.where(first, decay_end[:, h0:h0 + 1], decay_end[:, h0 + 1:h0 + 2])
            contrib = _dot_tn((xpair * de).astype(BF16), bg)
            cd = jnp.where(row_first, chunk_decay[h0:h0 + 1, :], chunk_decay[h0 + 1:h0 + 2, :])
            state_ref[pi] = s_pair * cd + contrib

    gw = D_INNER // SSD_GROUPS
    for g in range(SSD_GROUPS):
        cols = slice(g * gw, (g + 1) * gw)
        zg = z_ref[0, :, cols]
        yg = (ysc_ref[:, cols] + dsk_ref[:, cols] * act_ref[:, cols]) * (zg * _sigmoid(zg))
        ms = jnp.mean(yg * yg, axis=1, keepdims=True)
        y_ref[0, :, cols] = (yg * lax.rsqrt(ms + RMS_EPS) * nw_ref[:, cols]).astype(y_ref.dtype)

    @pl.when(c == nc - 1)
    def _():
        st_ref[0] = state_ref[...]


def _ssd_prompt(xbc3, z3, dt3, dtt3, conv_w, conv_b, a_log, d_lanes, norm_w):
    bsz, seq, _ = xbc3.shape
    q = SSD_CHUNK
    n_pairs = SSD_HEADS // 2
    const2 = lambda b, c: (0, 0)
    y, st = pl.pallas_call(
        _ssd_chunk_kernel,
        grid=(bsz, seq // q),
        in_specs=[pl.BlockSpec((1, q, CONV_DIM), lambda b, c: (b, c, 0)),
                  pl.BlockSpec((1, q, D_INNER), lambda b, c: (b, c, 0)),
                  pl.BlockSpec((1, q, SSD_HEADS), lambda b, c: (b, c, 0)),
                  pl.BlockSpec((1, SSD_HEADS, q), lambda b, c: (b, 0, c)),
                  pl.BlockSpec((CONV_K, CONV_DIM), const2),
                  pl.BlockSpec((1, CONV_DIM), const2),
                  pl.BlockSpec((1, SSD_HEADS), const2),
                  pl.BlockSpec((SSD_HEADS, 1), const2),
                  pl.BlockSpec((1, D_INNER), const2),
                  pl.BlockSpec((1, D_INNER), const2)],
        out_specs=[pl.BlockSpec((1, q, D_INNER), lambda b, c: (b, c, 0)),
                   pl.BlockSpec((1, n_pairs, 2 * SSD_HEAD_DIM, D_STATE), lambda b, c: (b, 0, 0, 0))],
        out_shape=[jax.ShapeDtypeStruct((bsz, seq, D_INNER), BF16),
                   jax.ShapeDtypeStruct((bsz, n_pairs, 2 * SSD_HEAD_DIM, D_STATE), F32)],
        scratch_shapes=[pltpu.VMEM((q + SUBLANES, CONV_DIM), F32),
                        pltpu.VMEM((q, CONV_DIM), F32),
                        pltpu.VMEM((q, D_INNER), F32),
                        pltpu.VMEM((n_pairs, 2 * SSD_HEAD_DIM, D_STATE), F32)],
        compiler_params=_cparams(("parallel", "arbitrary")),
        name="ssd_prompt",
    )(xbc3, z3, dt3, dtt3, conv_w, conv_b.reshape(1, CONV_DIM), a_log.reshape(1, SSD_HEADS),
      a_log.reshape(SSD_HEADS, 1), d_lanes, norm_w.reshape(1, D_INNER))
    return y, st.reshape(bsz, SSD_HEADS, SSD_HEAD_DIM, D_STATE)


def _ssd_step_kernel(xbc_ref, c0_ref, z_ref, dt_ref, s0_ref, cw_ref, cb_ref, al_ref, dsk_ref, nw_ref,
                     hexp_ref, gsum_ref, y_ref, s1_ref, xp_ref, xd_ref, bpad_ref):
    ld = xbc_ref.shape[1]
    tail = CONV_K - 1
    pair_w = 2 * SSD_HEAD_DIM
    gw = D_INNER // SSD_GROUPS

    xp_ref[SUBLANES - tail:SUBLANES, :] = c0_ref[0]
    xp_ref[SUBLANES:SUBLANES + ld, :] = xbc_ref[0]
    conv = cb_ref[...]
    for t in range(CONV_K):
        lo = SUBLANES - tail + t
        conv = conv + xp_ref[lo:lo + ld, :] * cw_ref[t:t + 1, :]
    act = conv * _sigmoid(conv)
    xs = act[:, :D_INNER]
    bm = act[:, D_INNER:D_INNER + GN]
    cm = act[:, D_INNER + GN:]

    dt = dt_ref[0]
    a = dt * (-jnp.exp(al_ref[...]))
    row = _iota((ld, SSD_HEADS), 0)
    acum = a
    sh = 1
    while sh < ld:
        acum = acum + jnp.where(row >= sh, pltpu.roll(acum, sh, 0), 0.0)
        sh *= 2
    a_end = acum[ld - 1:ld, :]
    hexp = hexp_ref[...]

    prod = jnp.concatenate([cm * bm[s:s + 1, :] for s in range(ld)], axis=0)
    cbh = jnp.dot(prod, gsum_ref[...], precision=HI, preferred_element_type=F32)
    a_l = jnp.concatenate([acum] * ld, axis=0)
    a_s = jnp.concatenate([jnp.broadcast_to(acum[s:s + 1, :], (ld, SSD_HEADS)) for s in range(ld)], axis=0)
    dt_s = jnp.concatenate([jnp.broadcast_to(dt[s:s + 1, :], (ld, SSD_HEADS)) for s in range(ld)], axis=0)
    pr_row = _iota((ld * ld, SSD_HEADS), 0)
    causal = (pr_row % ld) >= (pr_row // ld)
    w = cbh * jnp.exp(jnp.where(causal, a_l - a_s, -jnp.inf)) * dt_s
    wexp = jnp.dot(w, hexp, precision=HI, preferred_element_type=F32)
    y = jnp.zeros((ld, D_INNER), F32)
    for s in range(ld):
        y = y + wexp[s * ld:(s + 1) * ld, :] * xs[s:s + 1, :]

    ea = jnp.dot(jnp.exp(acum), hexp, precision=HI, preferred_element_type=F32)
    de = jnp.dot(jnp.exp(a_end - acum) * dt, hexp, precision=HI, preferred_element_type=F32)
    xd_ref[...] = jnp.zeros_like(xd_ref)
    bpad_ref[...] = jnp.zeros_like(bpad_ref)
    xd_ref[0:ld, :] = xs * de
    bpad_ref[0:ld, :] = bm
    chunk_decay = jnp.exp(a_end)
    pairs_per_group = SSD_HEADS_PER_GROUP // 2
    y_off = []
    for g in range(SSD_GROUPS):
        cg = cm[:, g * D_STATE:(g + 1) * D_STATE].astype(BF16)
        bg = bpad_ref[:, g * D_STATE:(g + 1) * D_STATE].astype(BF16)
        for pr in range(pairs_per_group):
            pi = g * pairs_per_group + pr
            s_pair = s0_ref[0, pi]
            y_off.append(_dot_nt(cg, s_pair.astype(BF16)))
            contrib = _dot_tn(xd_ref[:, pi * pair_w:(pi + 1) * pair_w].astype(BF16), bg)
            for hh in range(2):
                h = 2 * pi + hh
                rows = slice(hh * SSD_HEAD_DIM, (hh + 1) * SSD_HEAD_DIM)
                s1_ref[0, pi, rows, :] = s_pair[rows, :] * chunk_decay[0, h] + contrib[rows, :]
    y = y + jnp.concatenate(y_off, axis=1) * ea

    y = (y + dsk_ref[...] * xs) * (z_ref[0] * _sigmoid(z_ref[0]))
    for g in range(SSD_GROUPS):
        cols = slice(g * gw, (g + 1) * gw)
        yg = y[:, cols]
        ms = jnp.mean(yg * yg, axis=1, keepdims=True)
        y_ref[0, :, cols] = yg * lax.rsqrt(ms + RMS_EPS) * nw_ref[:, cols]


def _ssd_sample(xbc3, conv0, z3, dt3, state0, conv_w, conv_b, a_log, d_lanes, norm_w):
    bd, ld, _ = xbc3.shape
    n_pairs = SSD_HEADS // 2
    pair_w = 2 * SSD_HEAD_DIM
    hexp = (jnp.arange(D_INNER, dtype=I32)[None, :] // SSD_HEAD_DIM == jnp.arange(SSD_HEADS, dtype=I32)[:, None]).astype(F32)
    gsum = (jnp.arange(GN, dtype=I32)[:, None] // D_STATE ==
            jnp.arange(SSD_HEADS, dtype=I32)[None, :] // SSD_HEADS_PER_GROUP).astype(F32)
    s0 = state0.reshape(bd, n_pairs, pair_w, D_STATE)
    const2 = lambda b: (0, 0)
    y, s1 = pl.pallas_call(
        _ssd_step_kernel,
        grid=(bd,),
        in_specs=[pl.BlockSpec((1, ld, CONV_DIM), lambda b: (b, 0, 0)),
                  pl.BlockSpec((1, CONV_K - 1, CONV_DIM), lambda b: (b, 0, 0)),
                  pl.BlockSpec((1, ld, D_INNER), lambda b: (b, 0, 0)),
                  pl.BlockSpec((1, ld, SSD_HEADS), lambda b: (b, 0, 0)),
                  pl.BlockSpec((1, n_pairs, pair_w, D_STATE), lambda b: (b, 0, 0, 0)),
                  pl.BlockSpec((CONV_K, CONV_DIM), const2),
                  pl.BlockSpec((1, CONV_DIM), const2),
                  pl.BlockSpec((1, SSD_HEADS), const2),
                  pl.BlockSpec((1, D_INNER), const2),
                  pl.BlockSpec((1, D_INNER), const2),
                  pl.BlockSpec((SSD_HEADS, D_INNER), const2),
                  pl.BlockSpec((GN, SSD_HEADS), const2)],
        out_specs=[pl.BlockSpec((1, ld, D_INNER), lambda b: (b, 0, 0)),
                   pl.BlockSpec((1, n_pairs, pair_w, D_STATE), lambda b: (b, 0, 0, 0))],
        out_shape=[jax.ShapeDtypeStruct((bd, ld, D_INNER), F32),
                   jax.ShapeDtypeStruct((bd, n_pairs, pair_w, D_STATE), F32)],
        scratch_shapes=[pltpu.VMEM((SUBLANES + ld, CONV_DIM), F32),
                        pltpu.VMEM((LANES, D_INNER), F32),
                        pltpu.VMEM((LANES, GN), F32)],
        compiler_params=_cparams(("parallel",)),
        name="ssd_sample",
    )(xbc3, conv0, z3, dt3, s0, conv_w, conv_b.reshape(1, CONV_DIM), a_log.reshape(1, SSD_HEADS),
      d_lanes, norm_w.reshape(1, D_INNER), hexp, gsum)
    return y, s1.reshape(bd, SSD_HEADS, SSD_HEAD_DIM, D_STATE)


def _layer_norm(y, g, b):
    mu = jnp.mean(y, axis=1, keepdims=True)
    d = y - mu
    var = jnp.mean(d * d, axis=1, keepdims=True)
    return d * lax.rsqrt(var + LN_EPS) * g + b


def _post_mix_kernel(o_ref, w_ref, x_ref, g_ref, b_ref, wr_ref, br_ref, x1_ref, te_ref, tg_ref):
    mix = jnp.dot(o_ref[...].astype(BF16), w_ref[...], preferred_element_type=F32)
    x1 = _layer_norm(DEEPNORM_ALPHA * x_ref[...] + mix, g_ref[...], b_ref[...])
    x1_ref[...] = x1
    logits = jnp.dot(x1, wr_ref[...], precision=HI, preferred_element_type=F32) + br_ref[...]
    lane = _iota(logits.shape, 1)
    vals, idxs = [], []
    cur = logits
    for _ in range(TOP_K):
        mx = jnp.max(cur, axis=1, keepdims=True)
        idx = jnp.min(jnp.where(cur == mx, lane, LANES), axis=1, keepdims=True)
        vals.append(mx)
        idxs.append(idx)
        cur = jnp.where(lane == idx, -jnp.inf, cur)
    ex = [jnp.exp(v - vals[0]) for v in vals]
    den = ex[0] + ex[1] + ex[2] + ex[3]
    te = jnp.zeros(logits.shape, I32)
    tg = jnp.zeros(logits.shape, F32)
    for k in range(TOP_K):
        te = jnp.where(lane == k, idxs[k], te)
        tg = jnp.where(lane == k, ex[k] / den, tg)
    te_ref[...] = te
    tg_ref[...] = tg


def _post_mix(o, w_out, x, g, b, w_router, b_router, tm=256):
    m, k = o.shape
    wr = jnp.zeros((D_MODEL, LANES), F32).at[:, :N_EXPERTS].set(w_router)
    br = jnp.full((1, LANES), NEG_BIG, F32).at[0, :N_EXPERTS].set(b_router)
    row = lambda i: (i, 0)
    const = lambda i: (0, 0)
    return pl.pallas_call(
        _post_mix_kernel,
        grid=(m // tm,),
        in_specs=[pl.BlockSpec((tm, k), row),
                  pl.BlockSpec((k, D_MODEL), const),
                  pl.BlockSpec((tm, D_MODEL), row),
                  pl.BlockSpec((1, D_MODEL), const),
                  pl.BlockSpec((1, D_MODEL), const),
                  pl.BlockSpec((D_MODEL, LANES), const),
                  pl.BlockSpec((1, LANES), const)],
        out_specs=[pl.BlockSpec((tm, D_MODEL), row),
                   pl.BlockSpec((tm, LANES), row), pl.BlockSpec((tm, LANES), row)],
        out_shape=[jax.ShapeDtypeStruct((m, D_MODEL), F32),
                   jax.ShapeDtypeStruct((m, LANES), I32), jax.ShapeDtypeStruct((m, LANES), F32)],
        compiler_params=_cparams(("parallel",)),
        name="post_mix",
    )(o, w_out, x, g.reshape(1, D_MODEL), b.reshape(1, D_MODEL), wr, br)


def _rank_kernel(te_ref, rank_ref, cnt_ref, carry_ref):
    @pl.when(pl.program_id(0) == 0)
    def _():
        carry_ref[...] = jnp.zeros_like(carry_ref)
    te = te_ref[...]
    tr = te.shape[0]
    lane = _iota((tr, LANES), 1)
    onehot = jnp.zeros((tr, LANES), F32)
    for k in range(TOP_K):
        onehot = onehot + (lane == te[:, k:k + 1]).astype(F32)
    strict = (_iota((tr, tr), 0) > _iota((tr, tr), 1)).astype(BF16)
    before = jnp.dot(strict, onehot.astype(BF16), preferred_element_type=F32) + carry_ref[0:1, :]
    out = jnp.zeros((tr, LANES), F32)
    for k in range(TOP_K):
        rk = jnp.sum(jnp.where(lane == te[:, k:k + 1], before, 0.0), axis=1, keepdims=True)
        out = jnp.where(lane == k, rk, out)
    rank_ref[...] = out.astype(I32)
    total = carry_ref[0:1, :] + jnp.sum(onehot, axis=0, keepdims=True)
    carry_ref[...] = jnp.broadcast_to(total, carry_ref.shape)
    cnt_ref[...] = jnp.broadcast_to(total, cnt_ref.shape).astype(I32)


def _route_ranks(te, tr=512):
    m = te.shape[0]
    return pl.pallas_call(
        _rank_kernel,
        grid=(m // tr,),
        in_specs=[pl.BlockSpec((tr, LANES), lambda i: (i, 0))],
        out_specs=[pl.BlockSpec((tr, LANES), lambda i: (i, 0)),
                   pl.BlockSpec((SUBLANES, LANES), lambda i: (0, 0))],
        out_shape=[jax.ShapeDtypeStruct((m, LANES), I32), jax.ShapeDtypeStruct((SUBLANES, LANES), I32)],
        scratch_shapes=[pltpu.VMEM((SUBLANES, LANES), F32)],
        compiler_params=_cparams(("arbitrary",)),
        name="route_ranks",
    )(te)


def _expert_kernel(be_ref, nu_ref, x_ref, wgu_ref, bgu_ref, wd_ref, bd_ref, o_ref, wgu_b, wd_b):
    n = pl.program_id(0)
    prev = be_ref[jnp.maximum(n - 1, 0)]
    used = n < nu_ref[0]

    @pl.when(jnp.logical_and(used, jnp.logical_or(n == 0, be_ref[n] != prev)))
    def _():
        wgu_b[...] = wgu_ref[0, 0].astype(BF16)
        wd_b[...] = wd_ref[0, 0].astype(BF16)

    @pl.when(used)
    def _():
        h = jnp.dot(x_ref[...].astype(BF16), wgu_b[...], preferred_element_type=F32) + bgu_ref[0, 0]
        g = jnp.minimum(h[:, :D_FF], SWIGLU_LIMIT)
        u = jnp.clip(h[:, D_FF:], -SWIGLU_LIMIT, SWIGLU_LIMIT)
        act = g * _sigmoid(GLU_ALPHA * g) * (u + 1.0)
        o_ref[...] = jnp.dot(act.astype(BF16), wd_b[...], preferred_element_type=F32) + bd_ref[0, 0]

    @pl.when(jnp.logical_not(used))
    def _():
        o_ref[...] = jnp.zeros_like(o_ref)


def _experts(xb, block_e, n_used, w_gu, b_gu, w_down, b_down, layer):
    rows = xb.shape[0]
    nb = rows // MOE_ROWS

    def wmap(n, be, nu):
        return (layer, be[n], 0, 0)

    grid_spec = pltpu.PrefetchScalarGridSpec(
        num_scalar_prefetch=2,
        grid=(nb,),
        in_specs=[pl.BlockSpec((MOE_ROWS, D_MODEL), lambda n, be, nu: (n, 0)),
                  pl.BlockSpec((1, 1, D_MODEL, 2 * D_FF), wmap),
                  pl.BlockSpec((1, 1, 1, 2 * D_FF), wmap),
                  pl.BlockSpec((1, 1, D_FF, D_MODEL), wmap),
                  pl.BlockSpec((1, 1, 1, D_MODEL), wmap)],
        out_specs=pl.BlockSpec((MOE_ROWS, D_MODEL), lambda n, be, nu: (n, 0)),
        scratch_shapes=[pltpu.VMEM((D_MODEL, 2 * D_FF), BF16), pltpu.VMEM((D_FF, D_MODEL), BF16)])
    return pl.pallas_call(
        _expert_kernel,
        grid_spec=grid_spec,
        out_shape=jax.ShapeDtypeStruct((rows, D_MODEL), F32),
        compiler_params=_cparams(("arbitrary",)),
        name="experts",
    )(block_e, n_used, xb, w_gu, b_gu.reshape(DEPTH, N_EXPERTS, 1, 2 * D_FF), w_down,
      b_down.reshape(DEPTH, N_EXPERTS, 1, D_MODEL))


def _finish_kernel(x1_ref, yk_ref, tg_ref, g_ref, b_ref, wg_ref, p_ref, wp_ref, o_ref):
    tg = tg_ref[...]
    moe = jnp.zeros(x1_ref.shape, F32)
    for k in range(TOP_K):
        moe = moe + tg[:, k:k + 1] * yk_ref[k]
    x2 = _layer_norm(DEEPNORM_ALPHA * x1_ref[...] + moe, g_ref[...], b_ref[...])
    gate = _sigmoid(jnp.dot(x2.astype(BF16), wg_ref[...], preferred_element_type=F32))
    proj = jnp.dot(p_ref[...].astype(BF16), wp_ref[...], preferred_element_type=F32)
    o_ref[...] = x2 + gate * proj


def _finish(x1, yk, tg, g, b, w_pg, p, w_pp, tm=256):
    m = x1.shape[0]
    row = lambda i: (i, 0)
    const = lambda i: (0, 0)
    return pl.pallas_call(
        _finish_kernel,
        grid=(m // tm,),
        in_specs=[pl.BlockSpec((tm, D_MODEL), row),
                  pl.BlockSpec((TOP_K, tm, D_MODEL), lambda i: (0, i, 0)),
                  pl.BlockSpec((tm, LANES), row),
                  pl.BlockSpec((1, D_MODEL), const),
                  pl.BlockSpec((1, D_MODEL), const),
                  pl.BlockSpec((D_MODEL, D_MODEL), const),
                  pl.BlockSpec((tm, PLE_DIM), row),
                  pl.BlockSpec((PLE_DIM, D_MODEL), const)],
        out_specs=pl.BlockSpec((tm, D_MODEL), row),
        out_shape=jax.ShapeDtypeStruct((m, D_MODEL), F32),
        compiler_params=_cparams(("parallel",)),
        name="finish",
    )(x1, yk, tg, g.reshape(1, D_MODEL), b.reshape(1, D_MODEL), w_pg, p, w_pp)


def _moe_and_finish(layer, x1, te, tg, p_all, ln2_g, ln2_b, w_gu, b_gu, w_down, b_down, w_pg, w_pp):
    t = x1.shape[0]
    a = t * TOP_K
    rank, cnt = _route_ranks(te)
    counts = cnt[0, :N_EXPERTS]
    padded = (counts + MOE_ROWS - 1) // MOE_ROWS * MOE_ROWS
    pend = jnp.cumsum(padded)
    pstart = pend - padded
    e4 = te[:, :TOP_K]
    dest = pstart[e4] + rank[:, :TOP_K]
    nb = a // MOE_ROWS + N_EXPERTS
    block_start = jnp.arange(nb, dtype=I32) * MOE_ROWS
    block_e = jnp.minimum(jnp.sum((pend[None, :] <= block_start[:, None]).astype(I32), axis=1), N_EXPERTS - 1)
    n_used = (pend[-1:] // MOE_ROWS).astype(I32)
    tok_sorted = (jnp.argsort(e4.reshape(-1), stable=True) // TOP_K).astype(I32)
    start = jnp.cumsum(counts) - counts
    slot_e = jnp.repeat(block_e, MOE_ROWS)
    entry = jnp.arange(nb * MOE_ROWS, dtype=I32) - pstart[slot_e] + start[slot_e]
    slot_tok = tok_sorted[jnp.clip(entry, 0, a - 1)]
    xb = x1[slot_tok]
    yb = _experts(xb, block_e, n_used, w_gu, b_gu, w_down, b_down, layer)
    yk = yb[dest.T]
    return _finish(x1, yk, tg, ln2_g, ln2_b, w_pg, p_all, w_pp)


def kernel(x_prompt, x_sample, cache_k, cache_v, cache_logf, state_ssm, state_conv, page_table, p_prompt, p_sample,
           fox_w_in, fox_b_f, fox_w_out, ssd_w_in, ssd_conv_w, ssd_conv_b, ssd_dt_bias, ssd_a_log, ssd_d,
           ssd_norm_w, ssd_w_out, ln1_g, ln1_b, ln2_g, ln2_b, moe_w_router, moe_b_router, moe_w_gu, moe_b_gu,
           moe_w_down, moe_b_down, ple_w_gate, ple_w_proj):
    bsz, seq, _ = x_prompt.shape
    bd, ld, _ = x_sample.shape
    tp = bsz * seq
    ts = bd * ld
    x = jnp.concatenate([x_prompt.reshape(tp, D_MODEL), x_sample.reshape(ts, D_MODEL)], axis=0)
    p_all = jnp.concatenate([p_prompt.reshape(DEPTH, tp, PLE_DIM), p_sample.reshape(DEPTH, ts, PLE_DIM)], axis=1)

    prompt, sample = (0, tp), (tp, ts)
    kp, vp, lfp, hp, cp = [], [], [], [], []
    ks_, vs_, lfs, hs, cs = [], [], [], [], []
    for i in range(DEPTH):
        j = i // N_MIXERS
        if i % N_MIXERS == 0:
            w_in = fox_w_in[j]
            wq = w_in[:, :FOX_WIDTH].astype(BF16)
            wk = w_in[:, FOX_WIDTH:2 * FOX_WIDTH].astype(BF16)
            wv = w_in[:, 2 * FOX_WIDTH:3 * FOX_WIDTH].astype(BF16)
            wf = w_in[:, 3 * FOX_WIDTH:]
            k_p3 = _mm(x, wk, F32, prompt).reshape(bsz, seq, FOX_WIDTH)
            v_p3 = _mm(x, wv, F32, prompt).reshape(bsz, seq, FOX_WIDTH)
            k_s3 = _mm(x, wk, F32, sample).reshape(bd, ld, FOX_WIDTH)
            v_s3 = _mm(x, wv, F32, sample).reshape(bd, ld, FOX_WIDTH)
            lf_p3 = _gate_rows(x, wf, fox_b_f[j], "log_sigmoid", prompt).reshape(bsz, seq, FOX_HEADS)
            lf_s3 = _gate_rows(x, wf, fox_b_f[j], "log_sigmoid", sample).reshape(bd, ld, FOX_HEADS)
            q_p = _mm(x, wq, BF16, prompt, scale=FOX_SCALE * LOG2E).reshape(bsz, seq, FOX_WIDTH)
            q_s = _mm(x, wq, F32, sample, scale=FOX_SCALE * LOG2E).reshape(bd, ld, FOX_WIDTH)
            c4 = _gate_cols(x, bsz, seq, wf.T, fox_b_f[j], "log_sigmoid", True, LOG2E)
            c4 = c4.reshape(bsz, FOX_HEADS, 1, seq)
            o_p = _fox_attn(q_p, k_p3, v_p3, c4)
            o_s = _fox_decode(q_s, k_s3, v_s3, lf_s3, cache_k, cache_v, cache_logf, page_table, j)
            o_all = jnp.concatenate([o_p.reshape(tp, FOX_WIDTH), o_s.reshape(ts, FOX_WIDTH).astype(BF16)], axis=0)
            w_out = fox_w_out[j].astype(BF16)
            kp.append(k_p3.reshape(bsz, seq, FOX_HEADS, FOX_HEAD_DIM))
            vp.append(v_p3.reshape(bsz, seq, FOX_HEADS, FOX_HEAD_DIM))
            lfp.append(lf_p3)
            ks_.append(k_s3.reshape(bd, ld, FOX_HEADS, FOX_HEAD_DIM))
            vs_.append(v_s3.reshape(bd, ld, FOX_HEADS, FOX_HEAD_DIM))
            lfs.append(lf_s3)
        else:
            w_in = ssd_w_in[j]
            wz = w_in[:, :D_INNER].astype(BF16)
            wx = w_in[:, D_INNER:D_INNER + CONV_DIM].astype(BF16)
            wd = w_in[:, D_INNER + CONV_DIM:]
            z_p3 = _mm(x, wz, F32, prompt).reshape(bsz, seq, D_INNER)
            z_s3 = _mm(x, wz, F32, sample).reshape(bd, ld, D_INNER)
            xbc_p3 = _mm(x, wx, F32, prompt).reshape(bsz, seq, CONV_DIM)
            xbc_s3 = _mm(x, wx, F32, sample).reshape(bd, ld, CONV_DIM)
            dt_p3 = _gate_rows(x, wd, ssd_dt_bias[j], "softplus", prompt).reshape(bsz, seq, SSD_HEADS)
            dt_s3 = _gate_rows(x, wd, ssd_dt_bias[j], "softplus", sample).reshape(bd, ld, SSD_HEADS)
            dtt = _gate_cols(x, bsz, seq, wd.T, ssd_dt_bias[j], "softplus", False)
            d_lanes = jnp.repeat(ssd_d[j], SSD_HEAD_DIM).reshape(1, D_INNER)
            y_p, h_p = _ssd_prompt(xbc_p3, z_p3, dt_p3, dtt,
                                   ssd_conv_w[j], ssd_conv_b[j], ssd_a_log[j], d_lanes, ssd_norm_w[j])
            y_s, h_s = _ssd_sample(xbc_s3, state_conv[j], z_s3, dt_s3, state_ssm[j],
                                   ssd_conv_w[j], ssd_conv_b[j], ssd_a_log[j], d_lanes, ssd_norm_w[j])
            o_all = jnp.concatenate([y_p.reshape(tp, D_INNER), y_s.reshape(ts, D_INNER).astype(BF16)], axis=0)
            w_out = ssd_w_out[j].astype(BF16)
            tail = CONV_K - 1
            hp.append(h_p)
            cp.append(xbc_p3[:, seq - tail:, :])
            hs.append(h_s)
            cs.append(jnp.concatenate([state_conv[j], xbc_s3], axis=1)[:, ld:, :])
        x1, te, tg = _post_mix(o_all, w_out, x, ln1_g[i], ln1_b[i], moe_w_router[i], moe_b_router[i])
        x = _moe_and_finish(i, x1, te, tg, p_all[i], ln2_g[i], ln2_b[i], moe_w_gu, moe_b_gu,
                            moe_w_down, moe_b_down, ple_w_gate[i].astype(BF16), ple_w_proj[i].astype(BF16))
    y_prompt = x[:tp].reshape(bsz, seq, D_MODEL)
    y_sample = x[tp:].reshape(bd, ld, D_MODEL)
    return (y_prompt, y_sample, jnp.stack(kp), jnp.stack(vp), jnp.stack(lfp), jnp.stack(hp), jnp.stack(cp),
            jnp.stack(ks_), jnp.stack(vs_), jnp.stack(lfs), jnp.stack(hs), jnp.stack(cs))
```

```python
import functools

import jax
import jax.numpy as jnp
from jax import lax
from jax.experimental import pallas as pl
from jax.experimental.pallas import tpu as pltpu

F32 = jnp.float32
BF16 = jnp.bfloat16
I32 = jnp.int32
HI = lax.Precision.HIGHEST

D_MODEL = 1024
DEPTH = 4
PAGE_SIZE = 128
N_MIXERS = 2
FOX_HEAD_DIM = 64
FOX_HEADS = D_MODEL // FOX_HEAD_DIM
FOX_WIDTH = FOX_HEADS * FOX_HEAD_DIM
FOX_SCALE = FOX_HEAD_DIM ** -0.5
D_INNER = 2 * D_MODEL
SSD_HEAD_DIM = 64
SSD_HEADS = D_INNER // SSD_HEAD_DIM
SSD_GROUPS = 8
SSD_HEADS_PER_GROUP = SSD_HEADS // SSD_GROUPS
D_STATE = 128
CONV_K = 4
GN = SSD_GROUPS * D_STATE
CONV_DIM = D_INNER + 2 * GN
SSD_CHUNK = 128
RMS_EPS = 1e-5
N_EXPERTS = 32
TOP_K = 4
D_FF = D_MODEL
SWIGLU_LIMIT = 7.0
GLU_ALPHA = 1.702
PLE_DIM = 256
LN_EPS = 1e-5
DEEPNORM_ALPHA = (2 * DEPTH) ** 0.25

LANES = 128
SUBLANES = 8
MOE_ROWS = 256
VMEM_LIMIT = 56 * 1024 * 1024
NEG_BIG = -1e30
LOG2E = 1.4426950408889634
DECODE_PAGES_PER_STEP = 8


def _cparams(sem):
    return pltpu.CompilerParams(dimension_semantics=sem, vmem_limit_bytes=VMEM_LIMIT)


def _iota(shape, dim):
    return lax.broadcasted_iota(I32, shape, dim)


def _log_sigmoid(z):
    return jnp.minimum(z, 0.0) - jnp.log1p(jnp.exp(-jnp.abs(z)))


def _softplus(z):
    return jnp.maximum(z, 0.0) + jnp.log1p(jnp.exp(-jnp.abs(z)))


def _sigmoid(z):
    return 1.0 / (1.0 + jnp.exp(-z))


_ACT = {"log_sigmoid": _log_sigmoid, "softplus": _softplus}


def _dot_nt(a, b, **kw):
    return lax.dot_general(a, b, (((1,), (1,)), ((), ())), preferred_element_type=F32, **kw)


def _dot_tn(a, b, **kw):
    return lax.dot_general(a, b, (((0,), (0,)), ((), ())), preferred_element_type=F32, **kw)


def _mm_kernel(x_ref, w_ref, o_ref, *, scale):
    acc = jnp.dot(x_ref[...].astype(BF16), w_ref[...], preferred_element_type=F32)
    if scale != 1.0:
        acc = acc * scale
    o_ref[...] = acc.astype(o_ref.dtype)


def _mm(x, w, out_dtype, rows, scale=1.0, tm=512, tn=1024):
    k = x.shape[1]
    n = w.shape[1]
    row0, m = rows
    tm, tn = min(tm, m), min(tn, n)
    blk0 = row0 // tm
    return pl.pallas_call(
        functools.partial(_mm_kernel, scale=scale),
        grid=(m // tm, n // tn),
        in_specs=[pl.BlockSpec((tm, k), lambda i, j: (i + blk0, 0)),
                  pl.BlockSpec((k, tn), lambda i, j: (0, j))],
        out_specs=pl.BlockSpec((tm, tn), lambda i, j: (i, j)),
        out_shape=jax.ShapeDtypeStruct((m, n), out_dtype),
        compiler_params=_cparams(("parallel", "parallel")),
        name="mm",
    )(x, w)


def _mm_t_kernel(x_ref, wt_ref, o_ref):
    o_ref[0] = _dot_nt(wt_ref[...], x_ref[...].astype(BF16))


def _mm_t(x, wt, bsz, seq, tm=512):
    k = x.shape[1]
    n = wt.shape[0]
    tm = min(tm, seq)
    per_seq = seq // tm
    return pl.pallas_call(
        _mm_t_kernel,
        grid=(bsz * per_seq,),
        in_specs=[pl.BlockSpec((tm, k), lambda i: (i, 0)),
                  pl.BlockSpec((n, k), lambda i: (0, 0))],
        out_specs=pl.BlockSpec((1, n, tm), lambda i: (i // per_seq, 0, i % per_seq)),
        out_shape=jax.ShapeDtypeStruct((bsz, n, seq), F32),
        compiler_params=_cparams(("parallel",)),
        name="mm_t",
    )(x, wt)


def _gate_rows_kernel(x_ref, w_ref, b_ref, o_ref, *, kind):
    z = jnp.dot(x_ref[...], w_ref[...], precision=HI, preferred_element_type=F32) + b_ref[...]
    o_ref[...] = _ACT[kind](z)


def _gate_rows(x, w, b, kind, rows, tm=512):
    k = x.shape[1]
    n = w.shape[1]
    row0, m = rows
    blk0 = row0 // tm
    return pl.pallas_call(
        functools.partial(_gate_rows_kernel, kind=kind),
        grid=(m // tm,),
        in_specs=[pl.BlockSpec((tm, k), lambda i: (i + blk0, 0)),
                  pl.BlockSpec((k, n), lambda i: (0, 0)),
                  pl.BlockSpec((1, n), lambda i: (0, 0))],
        out_specs=pl.BlockSpec((tm, n), lambda i: (i, 0)),
        out_shape=jax.ShapeDtypeStruct((m, n), F32),
        compiler_params=_cparams(("parallel",)),
        name="gate_rows",
    )(x, w, b.reshape(1, n))


def _gate_cols_kernel(x_ref, wt_ref, b_ref, *refs, kind, cumsum, cum_scale):
    z = _dot_nt(wt_ref[...], x_ref[...], precision=HI) + b_ref[...]
    y = _ACT[kind](z)
    refs[0][0] = y
    if cumsum:
        c_ref, carry_ref = refs[1], refs[2]

        @pl.when(pl.program_id(1) == 0)
        def _():
            carry_ref[...] = jnp.zeros_like(carry_ref)
        tl = y.shape[1]
        upper = (_iota((tl, tl), 0) <= _iota((tl, tl), 1)).astype(F32)
        cs = jnp.dot(y, upper, precision=HI, preferred_element_type=F32) + carry_ref[...]
        c_ref[0] = cs * cum_scale
        carry_ref[...] = cs[:, tl - 1:tl]


def _gate_cols(x, bsz, seq, wt, b, kind, cumsum, cum_scale=1.0, tl=512):
    k = x.shape[1]
    n = wt.shape[0]
    tl = min(tl, seq)
    per_seq = seq // tl
    n_out = 2 if cumsum else 1
    out_spec = pl.BlockSpec((1, n, tl), lambda bi, li: (bi, 0, li))
    outs = pl.pallas_call(
        functools.partial(_gate_cols_kernel, kind=kind, cumsum=cumsum, cum_scale=cum_scale),
        grid=(bsz, per_seq),
        in_specs=[pl.BlockSpec((tl, k), lambda bi, li: (bi * per_seq + li, 0)),
                  pl.BlockSpec((n, k), lambda bi, li: (0, 0)),
                  pl.BlockSpec((n, 1), lambda bi, li: (0, 0))],
        out_specs=[out_spec] * n_out,
        out_shape=[jax.ShapeDtypeStruct((bsz, n, seq), F32)] * n_out,
        scratch_shapes=[pltpu.VMEM((n, 1), F32)] if cumsum else [],
        compiler_params=_cparams(("parallel", "arbitrary")),
        name="gate_cols",
    )(x, wt, b.reshape(n, 1))
    return tuple(outs) if cumsum else outs[0]


def _fox_attn_kernel(q_ref, k_ref, v_ref, c_ref, o_ref, kb_ref, vb_ref, *, tq, tk):
    qi = pl.program_id(2)
    seq = k_ref.shape[2]

    @pl.when(qi == 0)
    def _():
        def cast_block(i, carry):
            cols = pl.ds(pl.multiple_of(i * tq, tq), tq)
            kb_ref[:, cols] = k_ref[0, :, cols].astype(BF16)
            vb_ref[cols, :] = v_ref[0, :, cols].T.astype(BF16)
            return carry
        lax.fori_loop(0, seq // tq, cast_block, 0)

    q = q_ref[0]
    lane = _iota((tq, LANES), 1)
    first = lane < FOX_HEAD_DIM
    zero = jnp.zeros_like(q)
    qh = (jnp.where(first, q, zero), jnp.where(first, zero, q))

    def step(kb, carry, width, masked):
        rows = pl.ds(pl.multiple_of(kb * width, width), width)
        kblk = kb_ref[:, rows]
        vblk = vb_ref[rows, :]
        if masked:
            visible = (_iota((tq, width), 0) - _iota((tq, width), 1)) >= kb * width - qi * tq
        new = []
        for hh in range(2):
            m, l, acc = carry[hh]
            s = jnp.dot(qh[hh], kblk, preferred_element_type=F32) - c_ref[0, hh, :, rows]
            if masked:
                s = jnp.where(visible, s, -jnp.inf)
            m_new = jnp.maximum(m, jnp.max(s, axis=1, keepdims=True))
            alpha = jnp.exp2(m - m_new)
            p = jnp.exp2(s - m_new)
            l = alpha * l + jnp.sum(p, axis=1, keepdims=True)
            acc = alpha * acc + jnp.dot(p.astype(BF16), vblk, preferred_element_type=F32)
            new.append((m_new, l, acc))
        return tuple(new)

    init = (jnp.full((tq, 1), -jnp.inf, F32), jnp.zeros((tq, 1), F32), jnp.zeros((tq, LANES), F32))
    n_full = (qi * tq) // tk
    carry = lax.fori_loop(0, n_full, functools.partial(step, width=tk, masked=False), (init, init))
    carry = lax.fori_loop(n_full * (tk // tq), qi, functools.partial(step, width=tq, masked=False), carry)
    carry = step(qi, carry, tq, True)
    outs = [acc / l for _, l, acc in carry]
    o_ref[0] = jnp.where(first, outs[0], outs[1]).astype(o_ref.dtype)


def _fox_attn(q3, kt3, vt3, c4, tq=512, tk=1024):
    bsz, seq, width = q3.shape
    tq = min(tq, seq)
    tk = min(tk, seq)
    pairs = width // LANES
    return pl.pallas_call(
        functools.partial(_fox_attn_kernel, tq=tq, tk=tk),
        grid=(bsz, pairs, seq // tq),
        in_specs=[pl.BlockSpec((1, tq, LANES), lambda b, j, i: (b, i, j)),
                  pl.BlockSpec((1, LANES, seq), lambda b, j, i: (b, j, 0)),
                  pl.BlockSpec((1, LANES, seq), lambda b, j, i: (b, j, 0)),
                  pl.BlockSpec((1, 2, 1, seq), lambda b, j, i: (b, j, 0, 0))],
        out_specs=pl.BlockSpec((1, tq, LANES), lambda b, j, i: (b, i, j)),
        out_shape=jax.ShapeDtypeStruct((bsz, seq, width), BF16),
        scratch_shapes=[pltpu.VMEM((LANES, seq), BF16), pltpu.VMEM((seq, LANES), BF16)],
        compiler_params=_cparams(("parallel", "parallel", "arbitrary")),
        name="fox_attn",
    )(q3, kt3, vt3, c4)


def _fox_dec_kernel(*refs, pages_per_step):
    n = pages_per_step
    q_ref = refs[1]
    kc_refs = refs[2:2 + n]
    vc_refs = refs[2 + n:2 + 2 * n]
    lc_refs = refs[2 + 2 * n:2 + 3 * n]
    (kn_ref, vn_ref, ln_ref, o_ref,
     qbd_ref, kpad_ref, vpad_ref, m_ref, l_ref, acc_ref, carry_ref) = refs[2 + 3 * n:]
    step = pl.program_id(1)
    ld = q_ref.shape[1]
    rows = FOX_HEADS * ld
    width = q_ref.shape[2]

    @pl.when(step == 0)
    def _():
        q = q_ref[0]
        qt = jnp.broadcast_to(q[None], (FOX_HEADS, ld, width)).reshape(rows, width)
        head_of_row = _iota((rows, width), 0) // ld
        head_of_col = _iota((rows, width), 1) // FOX_HEAD_DIM
        qbd_ref[...] = jnp.where(head_of_row == head_of_col, qt, 0.0).astype(BF16)
        m_ref[...] = jnp.full(m_ref.shape, -jnp.inf, F32)
        l_ref[...] = jnp.zeros_like(l_ref)
        acc_ref[...] = jnp.zeros_like(acc_ref)
        carry_ref[...] = jnp.zeros_like(carry_ref)

    def attend(scores, lf_ts, pvs, causal):
        upper = (_iota((PAGE_SIZE, PAGE_SIZE), 0) <= _iota((PAGE_SIZE, PAGE_SIZE), 1)).astype(F32)
        offset = carry_ref[...]
        biased = []
        for s, lf_t in zip(scores, lf_ts):
            c = jnp.dot(lf_t, upper, precision=HI, preferred_element_type=F32)
            c_rows = jnp.broadcast_to(((c + offset) * LOG2E)[:, None, :], (FOX_HEADS, ld, PAGE_SIZE))
            biased.append(s - c_rows.reshape(rows, PAGE_SIZE))
            offset = offset + c[:, PAGE_SIZE - 1:PAGE_SIZE]
        carry_ref[...] = offset
        s = jnp.concatenate(biased, axis=1) if len(biased) > 1 else biased[0]
        if causal:
            tok = _iota(s.shape, 0) % ld
            key = _iota(s.shape, 1)
            s = jnp.where(key <= tok, s, -jnp.inf)
        m_old = m_ref[...]
        m_new = jnp.maximum(m_old, jnp.max(s, axis=1, keepdims=True))
        alpha = jnp.exp2(m_old - m_new)
        pr = jnp.exp2(s - m_new)
        l_ref[...] = alpha * l_ref[...] + jnp.sum(pr, axis=1, keepdims=True)
        pb = pr.astype(BF16)
        pv = pvs[0](pb[:, 0:PAGE_SIZE])
        for i in range(1, len(pvs)):
            pv = pv + pvs[i](pb[:, i * PAGE_SIZE:(i + 1) * PAGE_SIZE])
        acc_ref[...] = alpha * acc_ref[...] + pv
        m_ref[...] = m_new

    q = qbd_ref[...]
    scores = [jnp.dot(q, kc_refs[i][0, 0].astype(BF16), preferred_element_type=F32) for i in range(n)]
    pvs = [lambda p, i=i: _dot_nt(p, vc_refs[i][0, 0].astype(BF16)) for i in range(n)]
    attend(scores, [lc_refs[i][0, 0] for i in range(n)], pvs, False)

    @pl.when(step == pl.num_programs(1) - 1)
    def _():
        kpad_ref[...] = jnp.zeros_like(kpad_ref)
        vpad_ref[...] = jnp.zeros_like(vpad_ref)
        kpad_ref[0:ld, :] = kn_ref[0]
        vpad_ref[0:ld, :] = vn_ref[0]
        s = _dot_nt(qbd_ref[...], kpad_ref[...].astype(BF16))
        attend([s], [ln_ref[0]], [lambda p: jnp.dot(p, vpad_ref[...].astype(BF16), preferred_element_type=F32)], True)
        a3 = (acc_ref[...] / l_ref[...]).reshape(FOX_HEADS, ld, width)
        own = _iota((FOX_HEADS, ld, width), 0) == _iota((FOX_HEADS, ld, width), 2) // FOX_HEAD_DIM
        o_ref[0] = jnp.sum(jnp.where(own, a3, 0.0), axis=0)


def _fox_decode(q3, kn3, vn3, ln3, cache_k, cache_v, cache_lf, page_table, layer):
    bd, ld, width = q3.shape
    n_fox, n_pool = cache_k.shape[:2]
    n_pages = page_table.shape[1]
    rows = FOX_HEADS * ld
    pps = DECODE_PAGES_PER_STEP if n_pages % DECODE_PAGES_PER_STEP == 0 else 1
    n_steps = n_pages // pps
    kc_t = cache_k.transpose(0, 1, 3, 4, 2).reshape(n_fox, n_pool, width, PAGE_SIZE)
    vc_t = cache_v.transpose(0, 1, 3, 4, 2).reshape(n_fox, n_pool, width, PAGE_SIZE)
    lc_t = cache_lf.transpose(0, 1, 3, 2)
    ln_t = jnp.zeros((bd, FOX_HEADS, PAGE_SIZE), F32).at[:, :, :ld].set(ln3.transpose(0, 2, 1))

    def cache_map(i):
        return lambda b, s, pt: (layer, pt[b, s * pps + i], 0, 0)

    def seq_map(b, s, pt):
        return (b, 0, 0)

    grid_spec = pltpu.PrefetchScalarGridSpec(
        num_scalar_prefetch=1,
        grid=(bd, n_steps),
        in_specs=[pl.BlockSpec((1, ld, width), seq_map)]
        + [pl.BlockSpec((1, 1, width, PAGE_SIZE), cache_map(i)) for i in range(pps)]
        + [pl.BlockSpec((1, 1, width, PAGE_SIZE), cache_map(i)) for i in range(pps)]
        + [pl.BlockSpec((1, 1, FOX_HEADS, PAGE_SIZE), cache_map(i)) for i in range(pps)]
        + [pl.BlockSpec((1, ld, width), seq_map),
           pl.BlockSpec((1, ld, width), seq_map),
           pl.BlockSpec((1, FOX_HEADS, PAGE_SIZE), seq_map)],
        out_specs=pl.BlockSpec((1, ld, width), seq_map),
        scratch_shapes=[pltpu.VMEM((rows, width), BF16),
                        pltpu.VMEM((PAGE_SIZE, width), F32),
                        pltpu.VMEM((PAGE_SIZE, width), F32),
                        pltpu.VMEM((rows, 1), F32),
                        pltpu.VMEM((rows, 1), F32),
                        pltpu.VMEM((rows, width), F32),
                        pltpu.VMEM((FOX_HEADS, 1), F32)])
    return pl.pallas_call(
        functools.partial(_fox_dec_kernel, pages_per_step=pps),
        grid_spec=grid_spec,
        out_shape=jax.ShapeDtypeStruct((bd, ld, width), F32),
        compiler_params=_cparams(("parallel", "arbitrary")),
        name="fox_decode",
    )(page_table, q3, *([kc_t] * pps), *([vc_t] * pps), *([lc_t] * pps), kn3, vn3, ln_t)


def _ssd_chunk_kernel(xbc_ref, z_ref, dt_ref, dtt_ref, cw_ref, cb_ref, al_ref, alt_ref, dsk_ref, nw_ref,
                      y_ref, st_ref, xp_ref, act_ref, ysc_ref, state_ref):
    c = pl.program_id(1)
    nc = pl.num_programs(1)
    q = SSD_CHUNK
    pair_w = 2 * SSD_HEAD_DIM

    @pl.when(c == 0)
    def _():
        state_ref[...] = jnp.zeros_like(state_ref)
        xp_ref[0:SUBLANES, :] = jnp.zeros((SUBLANES, CONV_DIM), F32)

    xp_ref[SUBLANES:SUBLANES + q, :] = xbc_ref[0]
    conv = cb_ref[...]
    for t in range(CONV_K):
        lo = SUBLANES - (CONV_K - 1) + t
        conv = conv + xp_ref[lo:lo + q, :] * cw_ref[t:t + 1, :]
    xp_ref[0:SUBLANES, :] = xp_ref[q:q + SUBLANES, :]
    act_ref[...] = conv * _sigmoid(conv)

    dt = dt_ref[0]
    dtt = dtt_ref[0]
    a = dt * (-jnp.exp(al_ref[...]))
    at = dtt * (-jnp.exp(alt_ref[...]))
    lower = (_iota((q, q), 0) >= _iota((q, q), 1))
    acum = jnp.dot(lower.astype(F32), a, precision=HI, preferred_element_type=F32)
    acum_t = jnp.dot(at, (_iota((q, q), 0) <= _iota((q, q), 1)).astype(F32), precision=HI,
                     preferred_element_type=F32)
    a_end = acum[q - 1:q, :]
    decay_end = jnp.exp(a_end - acum) * dt
    exp_acum = jnp.exp(acum)
    chunk_decay = jnp.exp(acum_t[:, q - 1:q])

    lane = _iota((q, pair_w), 1)
    first = lane < SSD_HEAD_DIM
    row_first = _iota((pair_w, 1), 0) < SSD_HEAD_DIM
    for g in range(SSD_GROUPS):
        bg = act_ref[:, D_INNER + g * D_STATE:D_INNER + (g + 1) * D_STATE].astype(BF16)
        cg = act_ref[:, D_INNER + GN + g * D_STATE:D_INNER + GN + (g + 1) * D_STATE].astype(BF16)
        cb = _dot_nt(cg, bg)
        for pr in range(SSD_HEADS_PER_GROUP // 2):
            pi = g * (SSD_HEADS_PER_GROUP // 2) + pr
            h0 = 2 * pi
            xpair = act_ref[:, pi * pair_w:(pi + 1) * pair_w]
            xpair_b = xpair.astype(BF16)
            ys = []
            for hh in range(2):
                h = h0 + hh
                seg = acum[:, h:h + 1] - acum_t[h:h + 1, :]
                w = cb * jnp.exp(jnp.where(lower, seg, -jnp.inf)) * dtt[h:h + 1, :]
                ys.append(jnp.dot(w.astype(BF16), xpair_b, preferred_element_type=F32))
            y_diag = jnp.where(first, ys[0], ys[1])
            s_pair = state_ref[pi]
            ea = jnp.where(first, exp_acum[:, h0:h0 + 1], exp_acum[:, h0 + 1:h0 + 2])
            y_off = _dot_nt(cg, s_pair.astype(BF16)) * ea
            ysc_ref[:, pi * pair_w:(pi + 1) * pair_w] = y_diag + y_off
            de = jnp.where(first, decay_end[:, h0:h0 + 1], decay_end[:, h0 + 1:h0 + 2])
            contrib = _dot_tn((xpair * de).astype(BF16), bg)
            cd = jnp.where(row_first, chunk_decay[h0:h0 + 1, :], chunk_decay[h0 + 1:h0 + 2, :])
            state_ref[pi] = s_pair * cd + contrib

    gw = D_INNER // SSD_GROUPS
    for g in range(SSD_GROUPS):
        cols = slice(g * gw, (g + 1) * gw)
        zg = z_ref[0, :, cols]
        yg = (ysc_ref[:, cols] + dsk_ref[:, cols] * act_ref[:, cols]) * (zg * _sigmoid(zg))
        ms = jnp.mean(yg * yg, axis=1, keepdims=True)
        y_ref[0, :, cols] = (yg * lax.rsqrt(ms + RMS_EPS) * nw_ref[:, cols]).astype(y_ref.dtype)

    @pl.when(c == nc - 1)
    def _():
        st_ref[0] = state_ref[...]


def _ssd_prompt(xbc3, z3, dt3, dtt3, conv_w, conv_b, a_log, d_lanes, norm_w):
    bsz, seq, _ = xbc3.shape
    q = SSD_CHUNK
    n_pairs = SSD_HEADS // 2
    const2 = lambda b, c: (0, 0)
    y, st = pl.pallas_call(
        _ssd_chunk_kernel,
        grid=(bsz, seq // q),
        in_specs=[pl.BlockSpec((1, q, CONV_DIM), lambda b, c: (b, c, 0)),
                  pl.BlockSpec((1, q, D_INNER), lambda b, c: (b, c, 0)),
                  pl.BlockSpec((1, q, SSD_HEADS), lambda b, c: (b, c, 0)),
                  pl.BlockSpec((1, SSD_HEADS, q), lambda b, c: (b, 0, c)),
                  pl.BlockSpec((CONV_K, CONV_DIM), const2),
                  pl.BlockSpec((1, CONV_DIM), const2),
                  pl.BlockSpec((1, SSD_HEADS), const2),
                  pl.BlockSpec((SSD_HEADS, 1), const2),
                  pl.BlockSpec((1, D_INNER), const2),
                  pl.BlockSpec((1, D_INNER), const2)],
        out_specs=[pl.BlockSpec((1, q, D_INNER), lambda b, c: (b, c, 0)),
                   pl.BlockSpec((1, n_pairs, 2 * SSD_HEAD_DIM, D_STATE), lambda b, c: (b, 0, 0, 0))],
        out_shape=[jax.ShapeDtypeStruct((bsz, seq, D_INNER), BF16),
                   jax.ShapeDtypeStruct((bsz, n_pairs, 2 * SSD_HEAD_DIM, D_STATE), F32)],
        scratch_shapes=[pltpu.VMEM((q + SUBLANES, CONV_DIM), F32),
                        pltpu.VMEM((q, CONV_DIM), F32),
                        pltpu.VMEM((q, D_INNER), F32),
                        pltpu.VMEM((n_pairs, 2 * SSD_HEAD_DIM, D_STATE), F32)],
        compiler_params=_cparams(("parallel", "arbitrary")),
        name="ssd_prompt",
    )(xbc3, z3, dt3, dtt3, conv_w, conv_b.reshape(1, CONV_DIM), a_log.reshape(1, SSD_HEADS),
      a_log.reshape(SSD_HEADS, 1), d_lanes, norm_w.reshape(1, D_INNER))
    return y, st.reshape(bsz, SSD_HEADS, SSD_HEAD_DIM, D_STATE)


def _ssd_step_kernel(xbc_ref, c0_ref, z_ref, dt_ref, s0_ref, cw_ref, cb_ref, al_ref, dsk_ref, nw_ref,
                     hexp_ref, gsum_ref, y_ref, s1_ref, xp_ref, xd_ref, bpad_ref):
    ld = xbc_ref.shape[1]
    tail = CONV_K - 1
    pair_w = 2 * SSD_HEAD_DIM
    gw = D_INNER // SSD_GROUPS

    xp_ref[SUBLANES - tail:SUBLANES, :] = c0_ref[0]
    xp_ref[SUBLANES:SUBLANES + ld, :] = xbc_ref[0]
    conv = cb_ref[...]
    for t in range(CONV_K):
        lo = SUBLANES - tail + t
        conv = conv + xp_ref[lo:lo + ld, :] * cw_ref[t:t + 1, :]
    act = conv * _sigmoid(conv)
    xs = act[:, :D_INNER]
    bm = act[:, D_INNER:D_INNER + GN]
    cm = act[:, D_INNER + GN:]

    dt = dt_ref[0]
    a = dt * (-jnp.exp(al_ref[...]))
    row = _iota((ld, SSD_HEADS), 0)
    acum = a
    sh = 1
    while sh < ld:
        acum = acum + jnp.where(row >= sh, pltpu.roll(acum, sh, 0), 0.0)
        sh *= 2
    a_end = acum[ld - 1:ld, :]
    hexp = hexp_ref[...]

    prod = jnp.concatenate([cm * bm[s:s + 1, :] for s in range(ld)], axis=0)
    cbh = jnp.dot(prod, gsum_ref[...], precision=HI, preferred_element_type=F32)
    a_l = jnp.concatenate([acum] * ld, axis=0)
    a_s = jnp.concatenate([jnp.broadcast_to(acum[s:s + 1, :], (ld, SSD_HEADS)) for s in range(ld)], axis=0)
    dt_s = jnp.concatenate([jnp.broadcast_to(dt[s:s + 1, :], (ld, SSD_HEADS)) for s in range(ld)], axis=0)
    pr_row = _iota((ld * ld, SSD_HEADS), 0)
    causal = (pr_row % ld) >= (pr_row // ld)
    w = cbh * jnp.exp(jnp.where(causal, a_l - a_s, -jnp.inf)) * dt_s
    wexp = jnp.dot(w, hexp, precision=HI, preferred_element_type=F32)
    y = jnp.zeros((ld, D_INNER), F32)
    for s in range(ld):
        y = y + wexp[s * ld:(s + 1) * ld, :] * xs[s:s + 1, :]

    ea = jnp.dot(jnp.exp(acum), hexp, precision=HI, preferred_element_type=F32)
    de = jnp.dot(jnp.exp(a_end - acum) * dt, hexp, precision=HI, preferred_element_type=F32)
    xd_ref[...] = jnp.zeros_like(xd_ref)
    bpad_ref[...] = jnp.zeros_like(bpad_ref)
    xd_ref[0:ld, :] = xs * de
    bpad_ref[0:ld, :] = bm
    chunk_decay = jnp.exp(a_end)
    pairs_per_group = SSD_HEADS_PER_GROUP // 2
    y_off = []
    for g in range(SSD_GROUPS):
        cg = cm[:, g * D_STATE:(g + 1) * D_STATE].astype(BF16)
        bg = bpad_ref[:, g * D_STATE:(g + 1) * D_STATE].astype(BF16)
        for pr in range(pairs_per_group):
            pi = g * pairs_per_group + pr
            s_pair = s0_ref[0, pi]
            y_off.append(_dot_nt(cg, s_pair.astype(BF16)))
            contrib = _dot_tn(xd_ref[:, pi * pair_w:(pi + 1) * pair_w].astype(BF16), bg)
            for hh in range(2):
                h = 2 * pi + hh
                rows = slice(hh * SSD_HEAD_DIM, (hh + 1) * SSD_HEAD_DIM)
                s1_ref[0, pi, rows, :] = s_pair[rows, :] * chunk_decay[0, h] + contrib[rows, :]
    y = y + jnp.concatenate(y_off, axis=1) * ea

    y = (y + dsk_ref[...] * xs) * (z_ref[0] * _sigmoid(z_ref[0]))
    for g in range(SSD_GROUPS):
        cols = slice(g * gw, (g + 1) * gw)
        yg = y[:, cols]
        ms = jnp.mean(yg * yg, axis=1, keepdims=True)
        y_ref[0, :, cols] = yg * lax.rsqrt(ms + RMS_EPS) * nw_ref[:, cols]


def _ssd_sample(xbc3, conv0, z3, dt3, state0, conv_w, conv_b, a_log, d_lanes, norm_w):
    bd, ld, _ = xbc3.shape
    n_pairs = SSD_HEADS // 2
    pair_w = 2 * SSD_HEAD_DIM
    hexp = (jnp.arange(D_INNER, dtype=I32)[None, :] // SSD_HEAD_DIM == jnp.arange(SSD_HEADS, dtype=I32)[:, None]).astype(F32)
    gsum = (jnp.arange(GN, dtype=I32)[:, None] // D_STATE ==
            jnp.arange(SSD_HEADS, dtype=I32)[None, :] // SSD_HEADS_PER_GROUP).astype(F32)
    s0 = state0.reshape(bd, n_pairs, pair_w, D_STATE)
    const2 = lambda b: (0, 0)
    y, s1 = pl.pallas_call(
        _ssd_step_kernel,
        grid=(bd,),
        in_specs=[pl.BlockSpec((1, ld, CONV_DIM), lambda b: (b, 0, 0)),
                  pl.BlockSpec((1, CONV_K - 1, CONV_DIM), lambda b: (b, 0, 0)),
                  pl.BlockSpec((1, ld, D_INNER), lambda b: (b, 0, 0)),
                  pl.BlockSpec((1, ld, SSD_HEADS), lambda b: (b, 0, 0)),
                  pl.BlockSpec((1, n_pairs, pair_w, D_STATE), lambda b: (b, 0, 0, 0)),
                  pl.BlockSpec((CONV_K, CONV_DIM), const2),
                  pl.BlockSpec((1, CONV_DIM), const2),
                  pl.BlockSpec((1, SSD_HEADS), const2),
                  pl.BlockSpec((1, D_INNER), const2),
                  pl.BlockSpec((1, D_INNER), const2),
                  pl.BlockSpec((SSD_HEADS, D_INNER), const2),
                  pl.BlockSpec((GN, SSD_HEADS), const2)],
        out_specs=[pl.BlockSpec((1, ld, D_INNER), lambda b: (b, 0, 0)),
                   pl.BlockSpec((1, n_pairs, pair_w, D_STATE), lambda b: (b, 0, 0, 0))],
        out_shape=[jax.ShapeDtypeStruct((bd, ld, D_INNER), F32),
                   jax.ShapeDtypeStruct((bd, n_pairs, pair_w, D_STATE), F32)],
        scratch_shapes=[pltpu.VMEM((SUBLANES + ld, CONV_DIM), F32),
                        pltpu.VMEM((LANES, D_INNER), F32),
                        pltpu.VMEM((LANES, GN), F32)],
        compiler_params=_cparams(("parallel",)),
        name="ssd_sample",
    )(xbc3, conv0, z3, dt3, s0, conv_w, conv_b.reshape(1, CONV_DIM), a_log.reshape(1, SSD_HEADS),
      d_lanes, norm_w.reshape(1, D_INNER), hexp, gsum)
    return y, s1.reshape(bd, SSD_HEADS, SSD_HEAD_DIM, D_STATE)


def _layer_norm(y, g, b):
    mu = jnp.mean(y, axis=1, keepdims=True)
    d = y - mu
    var = jnp.mean(d * d, axis=1, keepdims=True)
    return d * lax.rsqrt(var + LN_EPS) * g + b


def _post_mix_kernel(op_ref, os_ref, w_ref, x_ref, g_ref, b_ref, wr_ref, br_ref, x1_ref, te_ref, tg_ref, *,
                     prompt_blocks):
    o = jnp.where(pl.program_id(0) < prompt_blocks, op_ref[...].astype(BF16), os_ref[...].astype(BF16))
    mix = jnp.dot(o, w_ref[...], preferred_element_type=F32)
    x1 = _layer_norm(DEEPNORM_ALPHA * x_ref[...] + mix, g_ref[...], b_ref[...])
    x1_ref[...] = x1
    logits = jnp.dot(x1, wr_ref[...], precision=HI, preferred_element_type=F32) + br_ref[...]
    lane = _iota(logits.shape, 1)
    vals, idxs = [], []
    cur = logits
    for _ in range(TOP_K):
        mx = jnp.max(cur, axis=1, keepdims=True)
        idx = jnp.min(jnp.where(cur == mx, lane, LANES), axis=1, keepdims=True)
        vals.append(mx)
        idxs.append(idx)
        cur = jnp.where(lane == idx, -jnp.inf, cur)
    ex = [jnp.exp(v - vals[0]) for v in vals]
    den = ex[0] + ex[1] + ex[2] + ex[3]
    te = jnp.zeros(logits.shape, I32)
    tg = jnp.zeros(logits.shape, F32)
    for k in range(TOP_K):
        te = jnp.where(lane == k, idxs[k], te)
        tg = jnp.where(lane == k, ex[k] / den, tg)
    te_ref[...] = te
    tg_ref[...] = tg


def _post_mix(o_p, o_s, w_out, x, g, b, w_router, b_router, tm=256):
    k = o_p.shape[1]
    m = x.shape[0]
    n_p = o_p.shape[0] // tm
    n_s = o_s.shape[0] // tm
    wr = jnp.zeros((D_MODEL, LANES), F32).at[:, :N_EXPERTS].set(w_router)
    br = jnp.full((1, LANES), NEG_BIG, F32).at[0, :N_EXPERTS].set(b_router)
    row = lambda i: (i, 0)
    const = lambda i: (0, 0)
    return pl.pallas_call(
        functools.partial(_post_mix_kernel, prompt_blocks=n_p),
        grid=(m // tm,),
        in_specs=[pl.BlockSpec((tm, k), lambda i: (jnp.minimum(i, n_p - 1), 0)),
                  pl.BlockSpec((tm, k), lambda i: (jnp.clip(i - n_p, 0, n_s - 1), 0)),
                  pl.BlockSpec((k, D_MODEL), const),
                  pl.BlockSpec((tm, D_MODEL), row),
                  pl.BlockSpec((1, D_MODEL), const),
                  pl.BlockSpec((1, D_MODEL), const),
                  pl.BlockSpec((D_MODEL, LANES), const),
                  pl.BlockSpec((1, LANES), const)],
        out_specs=[pl.BlockSpec((tm, D_MODEL), row),
                   pl.BlockSpec((tm, LANES), row), pl.BlockSpec((tm, LANES), row)],
        out_shape=[jax.ShapeDtypeStruct((m, D_MODEL), F32),
                   jax.ShapeDtypeStruct((m, LANES), I32), jax.ShapeDtypeStruct((m, LANES), F32)],
        compiler_params=_cparams(("parallel",)),
        name="post_mix",
    )(o_p, o_s, w_out, x, g.reshape(1, D_MODEL), b.reshape(1, D_MODEL), wr, br)


def _rank_kernel(te_ref, rank_ref, cnt_ref, carry_ref):
    @pl.when(pl.program_id(0) == 0)
    def _():
        carry_ref[...] = jnp.zeros_like(carry_ref)
    te = te_ref[...]
    tr = te.shape[0]
    lane = _iota((tr, LANES), 1)
    onehot = jnp.zeros((tr, LANES), F32)
    for k in range(TOP_K):
        onehot = onehot + (lane == te[:, k:k + 1]).astype(F32)
    strict = (_iota((tr, tr), 0) > _iota((tr, tr), 1)).astype(BF16)
    before = jnp.dot(strict, onehot.astype(BF16), preferred_element_type=F32) + carry_ref[0:1, :]
    out = jnp.zeros((tr, LANES), F32)
    for k in range(TOP_K):
        rk = jnp.sum(jnp.where(lane == te[:, k:k + 1], before, 0.0), axis=1, keepdims=True)
        out = jnp.where(lane == k, rk, out)
    rank_ref[...] = out.astype(I32)
    total = carry_ref[0:1, :] + jnp.sum(onehot, axis=0, keepdims=True)
    carry_ref[...] = jnp.broadcast_to(total, carry_ref.shape)
    cnt_ref[...] = jnp.broadcast_to(total, cnt_ref.shape).astype(I32)


def _route_ranks(te, tr=512):
    m = te.shape[0]
    return pl.pallas_call(
        _rank_kernel,
        grid=(m // tr,),
        in_specs=[pl.BlockSpec((tr, LANES), lambda i: (i, 0))],
        out_specs=[pl.BlockSpec((tr, LANES), lambda i: (i, 0)),
                   pl.BlockSpec((SUBLANES, LANES), lambda i: (0, 0))],
        out_shape=[jax.ShapeDtypeStruct((m, LANES), I32), jax.ShapeDtypeStruct((SUBLANES, LANES), I32)],
        scratch_shapes=[pltpu.VMEM((SUBLANES, LANES), F32)],
        compiler_params=_cparams(("arbitrary",)),
        name="route_ranks",
    )(te)


def _expert_kernel(be_ref, nu_ref, x_ref, wgu_ref, bgu_ref, wd_ref, bd_ref, o_ref, wgu_b, wd_b):
    n = pl.program_id(0)
    prev = be_ref[jnp.maximum(n - 1, 0)]
    used = n < nu_ref[0]

    @pl.when(jnp.logical_and(used, jnp.logical_or(n == 0, be_ref[n] != prev)))
    def _():
        wgu_b[...] = wgu_ref[0, 0].astype(BF16)
        wd_b[...] = wd_ref[0, 0].astype(BF16)

    @pl.when(used)
    def _():
        h = jnp.dot(x_ref[...].astype(BF16), wgu_b[...], preferred_element_type=F32) + bgu_ref[0, 0]
        g = jnp.minimum(h[:, :D_FF], SWIGLU_LIMIT)
        u = jnp.clip(h[:, D_FF:], -SWIGLU_LIMIT, SWIGLU_LIMIT)
        act = g * _sigmoid(GLU_ALPHA * g) * (u + 1.0)
        o_ref[...] = jnp.dot(act.astype(BF16), wd_b[...], preferred_element_type=F32) + bd_ref[0, 0]

    @pl.when(jnp.logical_not(used))
    def _():
        o_ref[...] = jnp.zeros_like(o_ref)


def _experts(xb, block_e, n_used, w_gu, b_gu, w_down, b_down, layer):
    rows = xb.shape[0]
    nb = rows // MOE_ROWS

    def wmap(n, be, nu):
        return (layer, be[n], 0, 0)

    grid_spec = pltpu.PrefetchScalarGridSpec(
        num_scalar_prefetch=2,
        grid=(nb,),
        in_specs=[pl.BlockSpec((MOE_ROWS, D_MODEL), lambda n, be, nu: (n, 0)),
                  pl.BlockSpec((1, 1, D_MODEL, 2 * D_FF), wmap),
                  pl.BlockSpec((1, 1, 1, 2 * D_FF), wmap),
                  pl.BlockSpec((1, 1, D_FF, D_MODEL), wmap),
                  pl.BlockSpec((1, 1, 1, D_MODEL), wmap)],
        out_specs=pl.BlockSpec((MOE_ROWS, D_MODEL), lambda n, be, nu: (n, 0)),
        scratch_shapes=[pltpu.VMEM((D_MODEL, 2 * D_FF), BF16), pltpu.VMEM((D_FF, D_MODEL), BF16)])
    return pl.pallas_call(
        _expert_kernel,
        grid_spec=grid_spec,
        out_shape=jax.ShapeDtypeStruct((rows, D_MODEL), F32),
        compiler_params=_cparams(("arbitrary",)),
        name="experts",
    )(block_e, n_used, xb, w_gu, b_gu.reshape(DEPTH, N_EXPERTS, 1, 2 * D_FF), w_down,
      b_down.reshape(DEPTH, N_EXPERTS, 1, D_MODEL))


def _finish_kernel(x1_ref, yk_ref, tg_ref, g_ref, b_ref, wg_ref, pp_ref, ps_ref, wp_ref, o_ref, *, prompt_blocks):
    tg = tg_ref[...]
    moe = jnp.zeros(x1_ref.shape, F32)
    for k in range(TOP_K):
        moe = moe + tg[:, k:k + 1] * yk_ref[k]
    x2 = _layer_norm(DEEPNORM_ALPHA * x1_ref[...] + moe, g_ref[...], b_ref[...])
    gate = _sigmoid(jnp.dot(x2.astype(BF16), wg_ref[...], preferred_element_type=F32))
    p = jnp.where(pl.program_id(0) < prompt_blocks, pp_ref[0], ps_ref[0])
    proj = jnp.dot(p.astype(BF16), wp_ref[...], preferred_element_type=F32)
    o_ref[...] = x2 + gate * proj


def _finish(x1, yk, tg, g, b, w_pg, p_p, p_s, layer, w_pp, tm=256):
    m = x1.shape[0]
    n_p = p_p.shape[1] // tm
    n_s = p_s.shape[1] // tm
    row = lambda i: (i, 0)
    const = lambda i: (0, 0)
    return pl.pallas_call(
        functools.partial(_finish_kernel, prompt_blocks=n_p),
        grid=(m // tm,),
        in_specs=[pl.BlockSpec((tm, D_MODEL), row),
                  pl.BlockSpec((TOP_K, tm, D_MODEL), lambda i: (0, i, 0)),
                  pl.BlockSpec((tm, LANES), row),
                  pl.BlockSpec((1, D_MODEL), const),
                  pl.BlockSpec((1, D_MODEL), const),
                  pl.BlockSpec((D_MODEL, D_MODEL), const),
                  pl.BlockSpec((1, tm, PLE_DIM), lambda i: (layer, jnp.minimum(i, n_p - 1), 0)),
                  pl.BlockSpec((1, tm, PLE_DIM), lambda i: (layer, jnp.clip(i - n_p, 0, n_s - 1), 0)),
                  pl.BlockSpec((PLE_DIM, D_MODEL), const)],
        out_specs=pl.BlockSpec((tm, D_MODEL), row),
        out_shape=jax.ShapeDtypeStruct((m, D_MODEL), F32),
        compiler_params=_cparams(("parallel",)),
        name="finish",
    )(x1, yk, tg, g.reshape(1, D_MODEL), b.reshape(1, D_MODEL), w_pg, p_p, p_s, w_pp)


def _moe_and_finish(layer, x1, te, tg, p_p, p_s, ln2_g, ln2_b, w_gu, b_gu, w_down, b_down, w_pg, w_pp):
    t = x1.shape[0]
    a = t * TOP_K
    rank, cnt = _route_ranks(te)
    counts = cnt[0, :N_EXPERTS]
    padded = (counts + MOE_ROWS - 1) // MOE_ROWS * MOE_ROWS
    pend = jnp.cumsum(padded)
    pstart = pend - padded
    e4 = te[:, :TOP_K]
    dest = pstart[e4] + rank[:, :TOP_K]
    nb = a // MOE_ROWS + N_EXPERTS
    block_start = jnp.arange(nb, dtype=I32) * MOE_ROWS
    block_e = jnp.minimum(jnp.sum((pend[None, :] <= block_start[:, None]).astype(I32), axis=1), N_EXPERTS - 1)
    n_used = (pend[-1:] // MOE_ROWS).astype(I32)
    flat = jnp.arange(a, dtype=I32)
    tok_sorted = (jnp.sort(e4.reshape(-1) * a + flat) % a) // TOP_K
    start = jnp.cumsum(counts) - counts
    slot_e = jnp.repeat(block_e, MOE_ROWS)
    entry = jnp.arange(nb * MOE_ROWS, dtype=I32) - pstart[slot_e] + start[slot_e]
    slot_tok = tok_sorted[jnp.clip(entry, 0, a - 1)]
    xb = x1[slot_tok]
    yb = _experts(xb, block_e, n_used, w_gu, b_gu, w_down, b_down, layer)
    yk = yb[dest.T]
    return _finish(x1, yk, tg, ln2_g, ln2_b, w_pg, p_p, p_s, layer, w_pp)


def kernel(x_prompt, x_sample, cache_k, cache_v, cache_logf, state_ssm, state_conv, page_table, p_prompt, p_sample,
           fox_w_in, fox_b_f, fox_w_out, ssd_w_in, ssd_conv_w, ssd_conv_b, ssd_dt_bias, ssd_a_log, ssd_d,
           ssd_norm_w, ssd_w_out, ln1_g, ln1_b, ln2_g, ln2_b, moe_w_router, moe_b_router, moe_w_gu, moe_b_gu,
           moe_w_down, moe_b_down, ple_w_gate, ple_w_proj):
    bsz, seq, _ = x_prompt.shape
    bd, ld, _ = x_sample.shape
    tp = bsz * seq
    ts = bd * ld
    x = jnp.concatenate([x_prompt.reshape(tp, D_MODEL), x_sample.reshape(ts, D_MODEL)], axis=0)
    p_p = p_prompt.reshape(DEPTH, tp, PLE_DIM)
    p_s = p_sample.reshape(DEPTH, ts, PLE_DIM)

    prompt, sample = (0, tp), (tp, ts)
    kp, vp, lfp, hp, cp = [], [], [], [], []
    ks_, vs_, lfs, hs, cs = [], [], [], [], []
    for i in range(DEPTH):
        j = i // N_MIXERS
        if i % N_MIXERS == 0:
            w_in = fox_w_in[j]
            wq = w_in[:, :FOX_WIDTH].astype(BF16)
            wk = w_in[:, FOX_WIDTH:2 * FOX_WIDTH].astype(BF16)
            wv = w_in[:, 2 * FOX_WIDTH:3 * FOX_WIDTH].astype(BF16)
            wf = w_in[:, 3 * FOX_WIDTH:]
            kt_p = _mm_t(x, wk.T, bsz, seq)
            vt_p = _mm_t(x, wv.T, bsz, seq)
            k_s3 = _mm(x, wk, F32, sample).reshape(bd, ld, FOX_WIDTH)
            v_s3 = _mm(x, wv, F32, sample).reshape(bd, ld, FOX_WIDTH)
            lf_s3 = _gate_rows(x, wf, fox_b_f[j], "log_sigmoid", sample).reshape(bd, ld, FOX_HEADS)
            q_p = _mm(x, wq, BF16, prompt, scale=FOX_SCALE * LOG2E).reshape(bsz, seq, FOX_WIDTH)
            q_s = _mm(x, wq, F32, sample, scale=FOX_SCALE * LOG2E).reshape(bd, ld, FOX_WIDTH)
            lft_p, c4 = _gate_cols(x, bsz, seq, wf.T, fox_b_f[j], "log_sigmoid", True, LOG2E)
            c4 = c4.reshape(bsz, FOX_HEADS, 1, seq)
            o_p = _fox_attn(q_p, kt_p, vt_p, c4)
            o_s = _fox_decode(q_s, k_s3, v_s3, lf_s3, cache_k, cache_v, cache_logf, page_table, j)
            o_p, o_s = o_p.reshape(tp, FOX_WIDTH), o_s.reshape(ts, FOX_WIDTH)
            w_out = fox_w_out[j].astype(BF16)
            kp.append(kt_p.reshape(bsz, FOX_HEADS, FOX_HEAD_DIM, seq).transpose(0, 3, 1, 2))
            vp.append(vt_p.reshape(bsz, FOX_HEADS, FOX_HEAD_DIM, seq).transpose(0, 3, 1, 2))
            lfp.append(lft_p.transpose(0, 2, 1))
            ks_.append(k_s3.reshape(bd, ld, FOX_HEADS, FOX_HEAD_DIM))
            vs_.append(v_s3.reshape(bd, ld, FOX_HEADS, FOX_HEAD_DIM))
            lfs.append(lf_s3)
        else:
            w_in = ssd_w_in[j]
            wz = w_in[:, :D_INNER].astype(BF16)
            wx = w_in[:, D_INNER:D_INNER + CONV_DIM].astype(BF16)
            wd = w_in[:, D_INNER + CONV_DIM:]
            z_p3 = _mm(x, wz, F32, prompt).reshape(bsz, seq, D_INNER)
            z_s3 = _mm(x, wz, F32, sample).reshape(bd, ld, D_INNER)
            xbc_p3 = _mm(x, wx, F32, prompt).reshape(bsz, seq, CONV_DIM)
            xbc_s3 = _mm(x, wx, F32, sample).reshape(bd, ld, CONV_DIM)
            dt_p3 = _gate_rows(x, wd, ssd_dt_bias[j], "softplus", prompt).reshape(bsz, seq, SSD_HEADS)
            dt_s3 = _gate_rows(x, wd, ssd_dt_bias[j], "softplus", sample).reshape(bd, ld, SSD_HEADS)
            dtt = _gate_cols(x, bsz, seq, wd.T, ssd_dt_bias[j], "softplus", False)
            d_lanes = jnp.repeat(ssd_d[j], SSD_HEAD_DIM).reshape(1, D_INNER)
            y_p, h_p = _ssd_prompt(xbc_p3, z_p3, dt_p3, dtt,
                                   ssd_conv_w[j], ssd_conv_b[j], ssd_a_log[j], d_lanes, ssd_norm_w[j])
            y_s, h_s = _ssd_sample(xbc_s3, state_conv[j], z_s3, dt_s3, state_ssm[j],
                                   ssd_conv_w[j], ssd_conv_b[j], ssd_a_log[j], d_lanes, ssd_norm_w[j])
            o_p, o_s = y_p.reshape(tp, D_INNER), y_s.reshape(ts, D_INNER)
            w_out = ssd_w_out[j].astype(BF16)
            tail = CONV_K - 1
            hp.append(h_p)
            cp.append(xbc_p3[:, seq - tail:, :])
            hs.append(h_s)
            cs.append(jnp.concatenate([state_conv[j], xbc_s3], axis=1)[:, ld:, :])
        x1, te, tg = _post_mix(o_p, o_s, w_out, x, ln1_g[i], ln1_b[i], moe_w_router[i], moe_b_router[i])
        x = _moe_and_finish(i, x1, te, tg, p_p, p_s, ln2_g[i], ln2_b[i], moe_w_gu, moe_b_gu,
                            moe_w_down, moe_b_down, ple_w_gate[i].astype(BF16), ple_w_proj[i].astype(BF16))
    y_prompt = x[:tp].reshape(bsz, seq, D_MODEL)
    y_sample = x[tp:].reshape(bd, ld, D_MODEL)
    return (y_prompt, y_sample, jnp.stack(kp), jnp.stack(vp), jnp.stack(lfp), jnp.stack(hp), jnp.stack(cp),
            jnp.stack(ks_), jnp.stack(vs_), jnp.stack(lfs), jnp.stack(hs), jnp.stack(cs))
```

```python
import functools

import jax
import jax.numpy as jnp
from jax import lax
from jax.experimental import pallas as pl
from jax.experimental.pallas import tpu as pltpu

F32 = jnp.float32
BF16 = jnp.bfloat16
I32 = jnp.int32
HI = lax.Precision.HIGHEST

D_MODEL = 1024
DEPTH = 4
PAGE_SIZE = 128
N_MIXERS = 2
FOX_HEAD_DIM = 64
FOX_HEADS = D_MODEL // FOX_HEAD_DIM
FOX_WIDTH = FOX_HEADS * FOX_HEAD_DIM
FOX_SCALE = FOX_HEAD_DIM ** -0.5
D_INNER = 2 * D_MODEL
SSD_HEAD_DIM = 64
SSD_HEADS = D_INNER // SSD_HEAD_DIM
SSD_GROUPS = 8
SSD_HEADS_PER_GROUP = SSD_HEADS // SSD_GROUPS
D_STATE = 128
CONV_K = 4
GN = SSD_GROUPS * D_STATE
CONV_DIM = D_INNER + 2 * GN
SSD_CHUNK = 128
RMS_EPS = 1e-5
N_EXPERTS = 32
TOP_K = 4
D_FF = D_MODEL
SWIGLU_LIMIT = 7.0
GLU_ALPHA = 1.702
PLE_DIM = 256
LN_EPS = 1e-5
DEEPNORM_ALPHA = (2 * DEPTH) ** 0.25

LANES = 128
SUBLANES = 8
MOE_ROWS = 256
VMEM_LIMIT = 56 * 1024 * 1024
NEG_BIG = -1e30
LOG2E = 1.4426950408889634
DECODE_PAGES_PER_STEP = 8


def _cparams(sem):
    return pltpu.CompilerParams(dimension_semantics=sem, vmem_limit_bytes=VMEM_LIMIT)


def _iota(shape, dim):
    return lax.broadcasted_iota(I32, shape, dim)


def _log_sigmoid(z):
    return jnp.minimum(z, 0.0) - jnp.log1p(jnp.exp(-jnp.abs(z)))


def _softplus(z):
    return jnp.maximum(z, 0.0) + jnp.log1p(jnp.exp(-jnp.abs(z)))


def _sigmoid(z):
    return 1.0 / (1.0 + jnp.exp(-z))


_ACT = {"log_sigmoid": _log_sigmoid, "softplus": _softplus}


def _dot_nt(a, b, **kw):
    return lax.dot_general(a, b, (((1,), (1,)), ((), ())), preferred_element_type=F32, **kw)


def _dot_tn(a, b, **kw):
    return lax.dot_general(a, b, (((0,), (0,)), ((), ())), preferred_element_type=F32, **kw)


def _mm_kernel(x_ref, w_ref, o_ref, *, scale):
    acc = jnp.dot(x_ref[...].astype(BF16), w_ref[...], preferred_element_type=F32)
    if scale != 1.0:
        acc = acc * scale
    o_ref[...] = acc.astype(o_ref.dtype)


def _mm(x, w, out_dtype, rows, scale=1.0, tm=512, tn=1024):
    k = x.shape[1]
    n = w.shape[1]
    row0, m = rows
    tm, tn = min(tm, m), min(tn, n)
    blk0 = row0 // tm
    return pl.pallas_call(
        functools.partial(_mm_kernel, scale=scale),
        grid=(m // tm, n // tn),
        in_specs=[pl.BlockSpec((tm, k), lambda i, j: (i + blk0, 0)),
                  pl.BlockSpec((k, tn), lambda i, j: (0, j))],
        out_specs=pl.BlockSpec((tm, tn), lambda i, j: (i, j)),
        out_shape=jax.ShapeDtypeStruct((m, n), out_dtype),
        compiler_params=_cparams(("parallel", "parallel")),
        name="mm",
    )(x, w)


def _mm_t_kernel(x_ref, wt_ref, o_ref):
    o_ref[0] = _dot_nt(wt_ref[...], x_ref[...].astype(BF16))


def _mm_t(x, wt, bsz, seq, tm=512):
    k = x.shape[1]
    n = wt.shape[0]
    tm = min(tm, seq)
    per_seq = seq // tm
    return pl.pallas_call(
        _mm_t_kernel,
        grid=(bsz * per_seq,),
        in_specs=[pl.BlockSpec((tm, k), lambda i: (i, 0)),
                  pl.BlockSpec((n, k), lambda i: (0, 0))],
        out_specs=pl.BlockSpec((1, n, tm), lambda i: (i // per_seq, 0, i % per_seq)),
        out_shape=jax.ShapeDtypeStruct((bsz, n, seq), F32),
        compiler_params=_cparams(("parallel",)),
        name="mm_t",
    )(x, wt)


def _gate_rows_kernel(x_ref, w_ref, b_ref, o_ref, *, kind):
    z = jnp.dot(x_ref[...], w_ref[...], precision=HI, preferred_element_type=F32) + b_ref[...]
    o_ref[...] = _ACT[kind](z)


def _gate_rows(x, w, b, kind, rows, tm=512):
    k = x.shape[1]
    n = w.shape[1]
    row0, m = rows
    blk0 = row0 // tm
    return pl.pallas_call(
        functools.partial(_gate_rows_kernel, kind=kind),
        grid=(m // tm,),
        in_specs=[pl.BlockSpec((tm, k), lambda i: (i + blk0, 0)),
                  pl.BlockSpec((k, n), lambda i: (0, 0)),
                  pl.BlockSpec((1, n), lambda i: (0, 0))],
        out_specs=pl.BlockSpec((tm, n), lambda i: (i, 0)),
        out_shape=jax.ShapeDtypeStruct((m, n), F32),
        compiler_params=_cparams(("parallel",)),
        name="gate_rows",
    )(x, w, b.reshape(1, n))


def _gate_cols_kernel(x_ref, wt_ref, b_ref, *refs, kind, cumsum, cum_scale):
    z = _dot_nt(wt_ref[...], x_ref[...], precision=HI) + b_ref[...]
    y = _ACT[kind](z)
    refs[0][0] = y
    if cumsum:
        c_ref, carry_ref = refs[1], refs[2]

        @pl.when(pl.program_id(1) == 0)
        def _():
            carry_ref[...] = jnp.zeros_like(carry_ref)
        tl = y.shape[1]
        upper = (_iota((tl, tl), 0) <= _iota((tl, tl), 1)).astype(F32)
        cs = jnp.dot(y, upper, precision=HI, preferred_element_type=F32) + carry_ref[...]
        c_ref[0] = cs * cum_scale
        carry_ref[...] = cs[:, tl - 1:tl]


def _gate_cols(x, bsz, seq, wt, b, kind, cumsum, cum_scale=1.0, tl=512):
    k = x.shape[1]
    n = wt.shape[0]
    tl = min(tl, seq)
    per_seq = seq // tl
    n_out = 2 if cumsum else 1
    out_spec = pl.BlockSpec((1, n, tl), lambda bi, li: (bi, 0, li))
    outs = pl.pallas_call(
        functools.partial(_gate_cols_kernel, kind=kind, cumsum=cumsum, cum_scale=cum_scale),
        grid=(bsz, per_seq),
        in_specs=[pl.BlockSpec((tl, k), lambda bi, li: (bi * per_seq + li, 0)),
                  pl.BlockSpec((n, k), lambda bi, li: (0, 0)),
                  pl.BlockSpec((n, 1), lambda bi, li: (0, 0))],
        out_specs=[out_spec] * n_out,
        out_shape=[jax.ShapeDtypeStruct((bsz, n, seq), F32)] * n_out,
        scratch_shapes=[pltpu.VMEM((n, 1), F32)] if cumsum else [],
        compiler_params=_cparams(("parallel", "arbitrary")),
        name="gate_cols",
    )(x, wt, b.reshape(n, 1))
    return tuple(outs) if cumsum else outs[0]


def _fox_attn_kernel(q_ref, k_ref, v_ref, c_ref, o_ref, kb_ref, vb_ref, *, tq, tk):
    qi = pl.program_id(2)
    seq = k_ref.shape[2]

    @pl.when(qi == 0)
    def _():
        def cast_block(i, carry):
            cols = pl.ds(pl.multiple_of(i * tq, tq), tq)
            kb_ref[:, cols] = k_ref[0, :, cols].astype(BF16)
            vb_ref[cols, :] = v_ref[0, :, cols].T.astype(BF16)
            return carry
        lax.fori_loop(0, seq // tq, cast_block, 0)

    q = q_ref[0]
    lane = _iota((tq, LANES), 1)
    first = lane < FOX_HEAD_DIM
    zero = jnp.zeros_like(q)
    qh = (jnp.where(first, q, zero), jnp.where(first, zero, q))

    def step(kb, carry, width, masked):
        rows = pl.ds(pl.multiple_of(kb * width, width), width)
        kblk = kb_ref[:, rows]
        vblk = vb_ref[rows, :]
        if masked:
            visible = (_iota((tq, width), 0) - _iota((tq, width), 1)) >= kb * width - qi * tq
        new = []
        for hh in range(2):
            m, l, acc = carry[hh]
            s = jnp.dot(qh[hh], kblk, preferred_element_type=F32) - c_ref[0, hh, :, rows]
            if masked:
                s = jnp.where(visible, s, -jnp.inf)
            m_new = jnp.maximum(m, jnp.max(s, axis=1, keepdims=True))
            alpha = jnp.exp2(m - m_new)
            p = jnp.exp2(s - m_new)
            l = alpha * l + jnp.sum(p, axis=1, keepdims=True)
            acc = alpha * acc + jnp.dot(p.astype(BF16), vblk, preferred_element_type=F32)
            new.append((m_new, l, acc))
        return tuple(new)

    init = (jnp.full((tq, 1), -jnp.inf, F32), jnp.zeros((tq, 1), F32), jnp.zeros((tq, LANES), F32))
    n_full = (qi * tq) // tk
    carry = lax.fori_loop(0, n_full, functools.partial(step, width=tk, masked=False), (init, init))
    carry = lax.fori_loop(n_full * (tk // tq), qi, functools.partial(step, width=tq, masked=False), carry)
    carry = step(qi, carry, tq, True)
    outs = [acc / l for _, l, acc in carry]
    o_ref[0] = jnp.where(first, outs[0], outs[1]).astype(o_ref.dtype)


def _fox_attn(q3, kt3, vt3, c4, tq=512, tk=1024):
    bsz, seq, width = q3.shape
    tq = min(tq, seq)
    tk = min(tk, seq)
    pairs = width // LANES
    return pl.pallas_call(
        functools.partial(_fox_attn_kernel, tq=tq, tk=tk),
        grid=(bsz, pairs, seq // tq),
        in_specs=[pl.BlockSpec((1, tq, LANES), lambda b, j, i: (b, i, j)),
                  pl.BlockSpec((1, LANES, seq), lambda b, j, i: (b, j, 0)),
                  pl.BlockSpec((1, LANES, seq), lambda b, j, i: (b, j, 0)),
                  pl.BlockSpec((1, 2, 1, seq), lambda b, j, i: (b, j, 0, 0))],
        out_specs=pl.BlockSpec((1, tq, LANES), lambda b, j, i: (b, i, j)),
        out_shape=jax.ShapeDtypeStruct((bsz, seq, width), BF16),
        scratch_shapes=[pltpu.VMEM((LANES, seq), BF16), pltpu.VMEM((seq, LANES), BF16)],
        compiler_params=_cparams(("parallel", "parallel", "arbitrary")),
        name="fox_attn",
    )(q3, kt3, vt3, c4)


def _fox_dec_kernel(*refs, pages_per_step):
    n = pages_per_step
    q_ref = refs[1]
    kc_refs = refs[2:2 + n]
    vc_refs = refs[2 + n:2 + 2 * n]
    lc_refs = refs[2 + 2 * n:2 + 3 * n]
    (kn_ref, vn_ref, ln_ref, o_ref,
     qbd_ref, kpad_ref, vpad_ref, m_ref, l_ref, acc_ref, carry_ref) = refs[2 + 3 * n:]
    step = pl.program_id(1)
    ld = q_ref.shape[1]
    rows = FOX_HEADS * ld
    width = q_ref.shape[2]

    @pl.when(step == 0)
    def _():
        q = q_ref[0]
        qt = jnp.broadcast_to(q[None], (FOX_HEADS, ld, width)).reshape(rows, width)
        head_of_row = _iota((rows, width), 0) // ld
        head_of_col = _iota((rows, width), 1) // FOX_HEAD_DIM
        qbd_ref[...] = jnp.where(head_of_row == head_of_col, qt, 0.0).astype(BF16)
        m_ref[...] = jnp.full(m_ref.shape, -jnp.inf, F32)
        l_ref[...] = jnp.zeros_like(l_ref)
        acc_ref[...] = jnp.zeros_like(acc_ref)
        carry_ref[...] = jnp.zeros_like(carry_ref)

    def attend(scores, lf_ts, pvs, causal):
        upper = (_iota((PAGE_SIZE, PAGE_SIZE), 0) <= _iota((PAGE_SIZE, PAGE_SIZE), 1)).astype(F32)
        offset = carry_ref[...]
        biased = []
        for s, lf_t in zip(scores, lf_ts):
            c = jnp.dot(lf_t, upper, precision=HI, preferred_element_type=F32)
            c_rows = jnp.broadcast_to(((c + offset) * LOG2E)[:, None, :], (FOX_HEADS, ld, PAGE_SIZE))
            biased.append(s - c_rows.reshape(rows, PAGE_SIZE))
            offset = offset + c[:, PAGE_SIZE - 1:PAGE_SIZE]
        carry_ref[...] = offset
        s = jnp.concatenate(biased, axis=1) if len(biased) > 1 else biased[0]
        if causal:
            tok = _iota(s.shape, 0) % ld
            key = _iota(s.shape, 1)
            s = jnp.where(key <= tok, s, -jnp.inf)
        m_old = m_ref[...]
        m_new = jnp.maximum(m_old, jnp.max(s, axis=1, keepdims=True))
        alpha = jnp.exp2(m_old - m_new)
        pr = jnp.exp2(s - m_new)
        l_ref[...] = alpha * l_ref[...] + jnp.sum(pr, axis=1, keepdims=True)
        pb = pr.astype(BF16)
        pv = pvs[0](pb[:, 0:PAGE_SIZE])
        for i in range(1, len(pvs)):
            pv = pv + pvs[i](pb[:, i * PAGE_SIZE:(i + 1) * PAGE_SIZE])
        acc_ref[...] = alpha * acc_ref[...] + pv
        m_ref[...] = m_new

    q = qbd_ref[...]
    scores = [jnp.dot(q, kc_refs[i][0, 0].astype(BF16), preferred_element_type=F32) for i in range(n)]
    pvs = [lambda p, i=i: _dot_nt(p, vc_refs[i][0, 0].astype(BF16)) for i in range(n)]
    attend(scores, [lc_refs[i][0, 0] for i in range(n)], pvs, False)

    @pl.when(step == pl.num_programs(1) - 1)
    def _():
        kpad_ref[...] = jnp.zeros_like(kpad_ref)
        vpad_ref[...] = jnp.zeros_like(vpad_ref)
        kpad_ref[0:ld, :] = kn_ref[0]
        vpad_ref[0:ld, :] = vn_ref[0]
        s = _dot_nt(qbd_ref[...], kpad_ref[...].astype(BF16))
        attend([s], [ln_ref[0]], [lambda p: jnp.dot(p, vpad_ref[...].astype(BF16), preferred_element_type=F32)], True)
        a3 = (acc_ref[...] / l_ref[...]).reshape(FOX_HEADS, ld, width)
        own = _iota((FOX_HEADS, ld, width), 0) == _iota((FOX_HEADS, ld, width), 2) // FOX_HEAD_DIM
        o_ref[0] = jnp.sum(jnp.where(own, a3, 0.0), axis=0)


def _fox_decode(q3, kn3, vn3, ln3, cache_k, cache_v, cache_lf, page_table, layer):
    bd, ld, width = q3.shape
    n_fox, n_pool = cache_k.shape[:2]
    n_pages = page_table.shape[1]
    rows = FOX_HEADS * ld
    pps = DECODE_PAGES_PER_STEP if n_pages % DECODE_PAGES_PER_STEP == 0 else 1
    n_steps = n_pages // pps
    kc_t = cache_k.transpose(0, 1, 3, 4, 2).reshape(n_fox, n_pool, width, PAGE_SIZE)
    vc_t = cache_v.transpose(0, 1, 3, 4, 2).reshape(n_fox, n_pool, width, PAGE_SIZE)
    lc_t = cache_lf.transpose(0, 1, 3, 2)
    ln_t = jnp.zeros((bd, FOX_HEADS, PAGE_SIZE), F32).at[:, :, :ld].set(ln3.transpose(0, 2, 1))

    def cache_map(i):
        return lambda b, s, pt: (layer, pt[b, s * pps + i], 0, 0)

    def seq_map(b, s, pt):
        return (b, 0, 0)

    grid_spec = pltpu.PrefetchScalarGridSpec(
        num_scalar_prefetch=1,
        grid=(bd, n_steps),
        in_specs=[pl.BlockSpec((1, ld, width), seq_map)]
        + [pl.BlockSpec((1, 1, width, PAGE_SIZE), cache_map(i)) for i in range(pps)]
        + [pl.BlockSpec((1, 1, width, PAGE_SIZE), cache_map(i)) for i in range(pps)]
        + [pl.BlockSpec((1, 1, FOX_HEADS, PAGE_SIZE), cache_map(i)) for i in range(pps)]
        + [pl.BlockSpec((1, ld, width), seq_map),
           pl.BlockSpec((1, ld, width), seq_map),
           pl.BlockSpec((1, FOX_HEADS, PAGE_SIZE), seq_map)],
        out_specs=pl.BlockSpec((1, ld, width), seq_map),
        scratch_shapes=[pltpu.VMEM((rows, width), BF16),
                        pltpu.VMEM((PAGE_SIZE, width), F32),
                        pltpu.VMEM((PAGE_SIZE, width), F32),
                        pltpu.VMEM((rows, 1), F32),
                        pltpu.VMEM((rows, 1), F32),
                        pltpu.VMEM((rows, width), F32),
                        pltpu.VMEM((FOX_HEADS, 1), F32)])
    return pl.pallas_call(
        functools.partial(_fox_dec_kernel, pages_per_step=pps),
        grid_spec=grid_spec,
        out_shape=jax.ShapeDtypeStruct((bd, ld, width), F32),
        compiler_params=_cparams(("parallel", "arbitrary")),
        name="fox_decode",
    )(page_table, q3, *([kc_t] * pps), *([vc_t] * pps), *([lc_t] * pps), kn3, vn3, ln_t)


def _ssd_chunk_kernel(xbc_ref, z_ref, dt_ref, dtt_ref, cw_ref, cb_ref, al_ref, alt_ref, dsk_ref, nw_ref,
                      y_ref, st_ref, xp_ref, act_ref, ysc_ref, state_ref):
    c = pl.program_id(1)
    nc = pl.num_programs(1)
    q = SSD_CHUNK
    pair_w = 2 * SSD_HEAD_DIM

    @pl.when(c == 0)
    def _():
        state_ref[...] = jnp.zeros_like(state_ref)
        xp_ref[0:SUBLANES, :] = jnp.zeros((SUBLANES, CONV_DIM), F32)

    xp_ref[SUBLANES:SUBLANES + q, :] = xbc_ref[0]
    conv = cb_ref[...]
    for t in range(CONV_K):
        lo = SUBLANES - (CONV_K - 1) + t
        conv = conv + xp_ref[lo:lo + q, :] * cw_ref[t:t + 1, :]
    xp_ref[0:SUBLANES, :] = xp_ref[q:q + SUBLANES, :]
    act_ref[...] = conv * _sigmoid(conv)

    dt = dt_ref[0]
    dtt = dtt_ref[0]
    a = dt * (-jnp.exp(al_ref[...]))
    at = dtt * (-jnp.exp(alt_ref[...]))
    lower = (_iota((q, q), 0) >= _iota((q, q), 1))
    acum = jnp.dot(lower.astype(F32), a, precision=HI, preferred_element_type=F32)
    acum_t = jnp.dot(at, (_iota((q, q), 0) <= _iota((q, q), 1)).astype(F32), precision=HI,
                     preferred_element_type=F32)
    a_end = acum[q - 1:q, :]
    decay_end = jnp.exp(a_end - acum) * dt
    exp_acum = jnp.exp(acum)
    chunk_decay = jnp.exp(acum_t[:, q - 1:q])

    lane = _iota((q, pair_w), 1)
    first = lane < SSD_HEAD_DIM
    row_first = _iota((pair_w, 1), 0) < SSD_HEAD_DIM
    for g in range(SSD_GROUPS):
        bg = act_ref[:, D_INNER + g * D_STATE:D_INNER + (g + 1) * D_STATE].astype(BF16)
        cg = act_ref[:, D_INNER + GN + g * D_STATE:D_INNER + GN + (g + 1) * D_STATE].astype(BF16)
        cb = _dot_nt(cg, bg)
        for pr in range(SSD_HEADS_PER_GROUP // 2):
            pi = g * (SSD_HEADS_PER_GROUP // 2) + pr
            h0 = 2 * pi
            xpair = act_ref[:, pi * pair_w:(pi + 1) * pair_w]
            xpair_b = xpair.astype(BF16)
            ys = []
            for hh in range(2):
                h = h0 + hh
                seg = acum[:, h:h + 1] - acum_t[h:h + 1, :]
                w = cb * jnp.exp(jnp.where(lower, seg, -jnp.inf)) * dtt[h:h + 1, :]
                ys.append(jnp.dot(w.astype(BF16), xpair_b, preferred_element_type=F32))
            y_diag = jnp.where(first, ys[0], ys[1])
            s_pair = state_ref[pi]
            ea = jnp.where(first, exp_acum[:, h0:h0 + 1], exp_acum[:, h0 + 1:h0 + 2])
            y_off = _dot_nt(cg, s_pair.astype(BF16)) * ea
            ysc_ref[:, pi * pair_w:(pi + 1) * pair_w] = y_diag + y_off
            de = jnp.where(first, decay_end[:, h0:h0 + 1], decay_end[:, h0 + 1:h0 + 2])
            contrib = _dot_tn((xpair * de).astype(BF16), bg)
            cd = jnp.where(row_first, chunk_decay[h0:h0 + 1, :], chunk_decay[h0 + 1:h0 + 2, :])
            state_ref[pi] = s_pair * cd + contrib

    gw = D_INNER // SSD_GROUPS
    for g in range(SSD_GROUPS):
        cols = slice(g * gw, (g + 1) * gw)
        zg = z_ref[0, :, cols]
        yg = (ysc_ref[:, cols] + dsk_ref[:, cols] * act_ref[:, cols]) * (zg * _sigmoid(zg))
        ms = jnp.mean(yg * yg, axis=1, keepdims=True)
        y_ref[0, :, cols] = (yg * lax.rsqrt(ms + RMS_EPS) * nw_ref[:, cols]).astype(y_ref.dtype)

    @pl.when(c == nc - 1)
    def _():
        st_ref[0] = state_ref[...]


def _ssd_prompt(xbc3, z3, dt3, dtt3, conv_w, conv_b, a_log, d_lanes, norm_w):
    bsz, seq, _ = xbc3.shape
    q = SSD_CHUNK
    n_pairs = SSD_HEADS // 2
    const2 = lambda b, c: (0, 0)
    y, st = pl.pallas_call(
        _ssd_chunk_kernel,
        grid=(bsz, seq // q),
        in_specs=[pl.BlockSpec((1, q, CONV_DIM), lambda b, c: (b, c, 0)),
                  pl.BlockSpec((1, q, D_INNER), lambda b, c: (b, c, 0)),
                  pl.BlockSpec((1, q, SSD_HEADS), lambda b, c: (b, c, 0)),
                  pl.BlockSpec((1, SSD_HEADS, q), lambda b, c: (b, 0, c)),
                  pl.BlockSpec((CONV_K, CONV_DIM), const2),
                  pl.BlockSpec((1, CONV_DIM), const2),
                  pl.BlockSpec((1, SSD_HEADS), const2),
                  pl.BlockSpec((SSD_HEADS, 1), const2),
                  pl.BlockSpec((1, D_INNER), const2),
                  pl.BlockSpec((1, D_INNER), const2)],
        out_specs=[pl.BlockSpec((1, q, D_INNER), lambda b, c: (b, c, 0)),
                   pl.BlockSpec((1, n_pairs, 2 * SSD_HEAD_DIM, D_STATE), lambda b, c: (b, 0, 0, 0))],
        out_shape=[jax.ShapeDtypeStruct((bsz, seq, D_INNER), BF16),
                   jax.ShapeDtypeStruct((bsz, n_pairs, 2 * SSD_HEAD_DIM, D_STATE), F32)],
        scratch_shapes=[pltpu.VMEM((q + SUBLANES, CONV_DIM), F32),
                        pltpu.VMEM((q, CONV_DIM), F32),
                        pltpu.VMEM((q, D_INNER), F32),
                        pltpu.VMEM((n_pairs, 2 * SSD_HEAD_DIM, D_STATE), F32)],
        compiler_params=_cparams(("parallel", "arbitrary")),
        name="ssd_prompt",
    )(xbc3, z3, dt3, dtt3, conv_w, conv_b.reshape(1, CONV_DIM), a_log.reshape(1, SSD_HEADS),
      a_log.reshape(SSD_HEADS, 1), d_lanes, norm_w.reshape(1, D_INNER))
    return y, st.reshape(bsz, SSD_HEADS, SSD_HEAD_DIM, D_STATE)


def _ssd_step_kernel(xbc_ref, c0_ref, z_ref, dt_ref, s0_ref, cw_ref, cb_ref, al_ref, dsk_ref, nw_ref,
                     hexp_ref, gsum_ref, y_ref, s1_ref, xp_ref, xd_ref, bpad_ref):
    ld = xbc_ref.shape[1]
    tail = CONV_K - 1
    pair_w = 2 * SSD_HEAD_DIM
    gw = D_INNER // SSD_GROUPS

    xp_ref[SUBLANES - tail:SUBLANES, :] = c0_ref[0]
    xp_ref[SUBLANES:SUBLANES + ld, :] = xbc_ref[0]
    conv = cb_ref[...]
    for t in range(CONV_K):
        lo = SUBLANES - tail + t
        conv = conv + xp_ref[lo:lo + ld, :] * cw_ref[t:t + 1, :]
    act = conv * _sigmoid(conv)
    xs = act[:, :D_INNER]
    bm = act[:, D_INNER:D_INNER + GN]
    cm = act[:, D_INNER + GN:]

    dt = dt_ref[0]
    a = dt * (-jnp.exp(al_ref[...]))
    row = _iota((ld, SSD_HEADS), 0)
    acum = a
    sh = 1
    while sh < ld:
        acum = acum + jnp.where(row >= sh, pltpu.roll(acum, sh, 0), 0.0)
        sh *= 2
    a_end = acum[ld - 1:ld, :]
    hexp = hexp_ref[...]

    prod = jnp.concatenate([cm * bm[s:s + 1, :] for s in range(ld)], axis=0)
    cbh = jnp.dot(prod, gsum_ref[...], precision=HI, preferred_element_type=F32)
    a_l = jnp.concatenate([acum] * ld, axis=0)
    a_s = jnp.concatenate([jnp.broadcast_to(acum[s:s + 1, :], (ld, SSD_HEADS)) for s in range(ld)], axis=0)
    dt_s = jnp.concatenate([jnp.broadcast_to(dt[s:s + 1, :], (ld, SSD_HEADS)) for s in range(ld)], axis=0)
    pr_row = _iota((ld * ld, SSD_HEADS), 0)
    causal = (pr_row % ld) >= (pr_row // ld)
    w = cbh * jnp.exp(jnp.where(causal, a_l - a_s, -jnp.inf)) * dt_s
    wexp = jnp.dot(w, hexp, precision=HI, preferred_element_type=F32)
    y = jnp.zeros((ld, D_INNER), F32)
    for s in range(ld):
        y = y + wexp[s * ld:(s + 1) * ld, :] * xs[s:s + 1, :]

    ea = jnp.dot(jnp.exp(acum), hexp, precision=HI, preferred_element_type=F32)
    de = jnp.dot(jnp.exp(a_end - acum) * dt, hexp, precision=HI, preferred_element_type=F32)
    xd_ref[...] = jnp.zeros_like(xd_ref)
    bpad_ref[...] = jnp.zeros_like(bpad_ref)
    xd_ref[0:ld, :] = xs * de
    bpad_ref[0:ld, :] = bm
    chunk_decay = jnp.exp(a_end)
    pairs_per_group = SSD_HEADS_PER_GROUP // 2
    y_off = []
    for g in range(SSD_GROUPS):
        cg = cm[:, g * D_STATE:(g + 1) * D_STATE].astype(BF16)
        bg = bpad_ref[:, g * D_STATE:(g + 1) * D_STATE].astype(BF16)
        for pr in range(pairs_per_group):
            pi = g * pairs_per_group + pr
            s_pair = s0_ref[0, pi]
            y_off.append(_dot_nt(cg, s_pair.astype(BF16)))
            contrib = _dot_tn(xd_ref[:, pi * pair_w:(pi + 1) * pair_w].astype(BF16), bg)
            for hh in range(2):
                h = 2 * pi + hh
                rows = slice(hh * SSD_HEAD_DIM, (hh + 1) * SSD_HEAD_DIM)
                s1_ref[0, pi, rows, :] = s_pair[rows, :] * chunk_decay[0, h] + contrib[rows, :]
    y = y + jnp.concatenate(y_off, axis=1) * ea

    y = (y + dsk_ref[...] * xs) * (z_ref[0] * _sigmoid(z_ref[0]))
    for g in range(SSD_GROUPS):
        cols = slice(g * gw, (g + 1) * gw)
        yg = y[:, cols]
        ms = jnp.mean(yg * yg, axis=1, keepdims=True)
        y_ref[0, :, cols] = yg * lax.rsqrt(ms + RMS_EPS) * nw_ref[:, cols]


def _ssd_sample(xbc3, conv0, z3, dt3, state_all, layer, conv_w, conv_b, a_log, d_lanes, norm_w):
    bd, ld, _ = xbc3.shape
    n_pairs = SSD_HEADS // 2
    pair_w = 2 * SSD_HEAD_DIM
    hexp = (jnp.arange(D_INNER, dtype=I32)[None, :] // SSD_HEAD_DIM == jnp.arange(SSD_HEADS, dtype=I32)[:, None]).astype(F32)
    gsum = (jnp.arange(GN, dtype=I32)[:, None] // D_STATE ==
            jnp.arange(SSD_HEADS, dtype=I32)[None, :] // SSD_HEADS_PER_GROUP).astype(F32)
    s0 = state_all.reshape(state_all.shape[0] * bd, n_pairs, pair_w, D_STATE)
    const2 = lambda b: (0, 0)
    y, s1 = pl.pallas_call(
        _ssd_step_kernel,
        grid=(bd,),
        in_specs=[pl.BlockSpec((1, ld, CONV_DIM), lambda b: (b, 0, 0)),
                  pl.BlockSpec((1, CONV_K - 1, CONV_DIM), lambda b: (b, 0, 0)),
                  pl.BlockSpec((1, ld, D_INNER), lambda b: (b, 0, 0)),
                  pl.BlockSpec((1, ld, SSD_HEADS), lambda b: (b, 0, 0)),
                  pl.BlockSpec((1, n_pairs, pair_w, D_STATE), lambda b: (layer * bd + b, 0, 0, 0)),
                  pl.BlockSpec((CONV_K, CONV_DIM), const2),
                  pl.BlockSpec((1, CONV_DIM), const2),
                  pl.BlockSpec((1, SSD_HEADS), const2),
                  pl.BlockSpec((1, D_INNER), const2),
                  pl.BlockSpec((1, D_INNER), const2),
                  pl.BlockSpec((SSD_HEADS, D_INNER), const2),
                  pl.BlockSpec((GN, SSD_HEADS), const2)],
        out_specs=[pl.BlockSpec((1, ld, D_INNER), lambda b: (b, 0, 0)),
                   pl.BlockSpec((1, n_pairs, pair_w, D_STATE), lambda b: (b, 0, 0, 0))],
        out_shape=[jax.ShapeDtypeStruct((bd, ld, D_INNER), F32),
                   jax.ShapeDtypeStruct((bd, n_pairs, pair_w, D_STATE), F32)],
        scratch_shapes=[pltpu.VMEM((SUBLANES + ld, CONV_DIM), F32),
                        pltpu.VMEM((LANES, D_INNER), F32),
                        pltpu.VMEM((LANES, GN), F32)],
        compiler_params=_cparams(("parallel",)),
        name="ssd_sample",
    )(xbc3, conv0, z3, dt3, s0, conv_w, conv_b.reshape(1, CONV_DIM), a_log.reshape(1, SSD_HEADS),
      d_lanes, norm_w.reshape(1, D_INNER), hexp, gsum)
    return y, s1.reshape(bd, SSD_HEADS, SSD_HEAD_DIM, D_STATE)


def _layer_norm(y, g, b):
    mu = jnp.mean(y, axis=1, keepdims=True)
    d = y - mu
    var = jnp.mean(d * d, axis=1, keepdims=True)
    return d * lax.rsqrt(var + LN_EPS) * g + b


def _post_mix_kernel(op_ref, os_ref, w_ref, x_ref, g_ref, b_ref, wr_ref, br_ref, x1_ref, te_ref, tg_ref, *,
                     prompt_blocks):
    o = jnp.where(pl.program_id(0) < prompt_blocks, op_ref[...].astype(BF16), os_ref[...].astype(BF16))
    mix = jnp.dot(o, w_ref[...], preferred_element_type=F32)
    x1 = _layer_norm(DEEPNORM_ALPHA * x_ref[...] + mix, g_ref[...], b_ref[...])
    x1_ref[...] = x1
    logits = jnp.dot(x1, wr_ref[...], precision=HI, preferred_element_type=F32) + br_ref[...]
    lane = _iota(logits.shape, 1)
    vals, idxs = [], []
    cur = logits
    for _ in range(TOP_K):
        mx = jnp.max(cur, axis=1, keepdims=True)
        idx = jnp.min(jnp.where(cur == mx, lane, LANES), axis=1, keepdims=True)
        vals.append(mx)
        idxs.append(idx)
        cur = jnp.where(lane == idx, -jnp.inf, cur)
    ex = [jnp.exp(v - vals[0]) for v in vals]
    den = ex[0] + ex[1] + ex[2] + ex[3]
    te = jnp.zeros(logits.shape, I32)
    tg = jnp.zeros(logits.shape, F32)
    for k in range(TOP_K):
        te = jnp.where(lane == k, idxs[k], te)
        tg = jnp.where(lane == k, ex[k] / den, tg)
    te_ref[...] = te
    tg_ref[...] = tg


def _post_mix(o_p, o_s, w_out, x, g, b, w_router, b_router, tm=512):
    k = o_p.shape[1]
    m = x.shape[0]
    n_p = o_p.shape[0] // tm
    n_s = o_s.shape[0] // tm
    wr = jnp.zeros((D_MODEL, LANES), F32).at[:, :N_EXPERTS].set(w_router)
    br = jnp.full((1, LANES), NEG_BIG, F32).at[0, :N_EXPERTS].set(b_router)
    row = lambda i: (i, 0)
    const = lambda i: (0, 0)
    return pl.pallas_call(
        functools.partial(_post_mix_kernel, prompt_blocks=n_p),
        grid=(m // tm,),
        in_specs=[pl.BlockSpec((tm, k), lambda i: (jnp.minimum(i, n_p - 1), 0)),
                  pl.BlockSpec((tm, k), lambda i: (jnp.clip(i - n_p, 0, n_s - 1), 0)),
                  pl.BlockSpec((k, D_MODEL), const),
                  pl.BlockSpec((tm, D_MODEL), row),
                  pl.BlockSpec((1, D_MODEL), const),
                  pl.BlockSpec((1, D_MODEL), const),
                  pl.BlockSpec((D_MODEL, LANES), const),
                  pl.BlockSpec((1, LANES), const)],
        out_specs=[pl.BlockSpec((tm, D_MODEL), row),
                   pl.BlockSpec((tm, LANES), row), pl.BlockSpec((tm, LANES), row)],
        out_shape=[jax.ShapeDtypeStruct((m, D_MODEL), F32),
                   jax.ShapeDtypeStruct((m, LANES), I32), jax.ShapeDtypeStruct((m, LANES), F32)],
        compiler_params=_cparams(("parallel",)),
        name="post_mix",
    )(o_p, o_s, w_out, x, g.reshape(1, D_MODEL), b.reshape(1, D_MODEL), wr, br)


def _rank_kernel(te_ref, rank_ref, cnt_ref, carry_ref):
    @pl.when(pl.program_id(0) == 0)
    def _():
        carry_ref[...] = jnp.zeros_like(carry_ref)
    te = te_ref[...]
    tr = te.shape[0]
    lane = _iota((tr, LANES), 1)
    onehot = jnp.zeros((tr, LANES), F32)
    for k in range(TOP_K):
        onehot = onehot + (lane == te[:, k:k + 1]).astype(F32)
    strict = (_iota((tr, tr), 0) > _iota((tr, tr), 1)).astype(BF16)
    before = jnp.dot(strict, onehot.astype(BF16), preferred_element_type=F32) + carry_ref[0:1, :]
    out = jnp.zeros((tr, LANES), F32)
    for k in range(TOP_K):
        rk = jnp.sum(jnp.where(lane == te[:, k:k + 1], before, 0.0), axis=1, keepdims=True)
        out = jnp.where(lane == k, rk, out)
    rank_ref[...] = out.astype(I32)
    total = carry_ref[0:1, :] + jnp.sum(onehot, axis=0, keepdims=True)
    carry_ref[...] = jnp.broadcast_to(total, carry_ref.shape)
    cnt_ref[...] = jnp.broadcast_to(total, cnt_ref.shape).astype(I32)


def _route_ranks(te, tr=512):
    m = te.shape[0]
    return pl.pallas_call(
        _rank_kernel,
        grid=(m // tr,),
        in_specs=[pl.BlockSpec((tr, LANES), lambda i: (i, 0))],
        out_specs=[pl.BlockSpec((tr, LANES), lambda i: (i, 0)),
                   pl.BlockSpec((SUBLANES, LANES), lambda i: (0, 0))],
        out_shape=[jax.ShapeDtypeStruct((m, LANES), I32), jax.ShapeDtypeStruct((SUBLANES, LANES), I32)],
        scratch_shapes=[pltpu.VMEM((SUBLANES, LANES), F32)],
        compiler_params=_cparams(("arbitrary",)),
        name="route_ranks",
    )(te)


def _expert_kernel(be_ref, nu_ref, x_ref, wgu_ref, bgu_ref, wd_ref, bd_ref, o_ref, wgu_b, wd_b):
    n = pl.program_id(0)
    prev = be_ref[jnp.maximum(n - 1, 0)]
    used = n < nu_ref[0]

    @pl.when(jnp.logical_and(used, jnp.logical_or(n == 0, be_ref[n] != prev)))
    def _():
        wgu_b[...] = wgu_ref[0, 0].astype(BF16)
        wd_b[...] = wd_ref[0, 0].astype(BF16)

    @pl.when(used)
    def _():
        h = jnp.dot(x_ref[...].astype(BF16), wgu_b[...], preferred_element_type=F32) + bgu_ref[0, 0]
        g = jnp.minimum(h[:, :D_FF], SWIGLU_LIMIT)
        u = jnp.clip(h[:, D_FF:], -SWIGLU_LIMIT, SWIGLU_LIMIT)
        act = g * _sigmoid(GLU_ALPHA * g) * (u + 1.0)
        o_ref[...] = jnp.dot(act.astype(BF16), wd_b[...], preferred_element_type=F32) + bd_ref[0, 0]

    @pl.when(jnp.logical_not(used))
    def _():
        o_ref[...] = jnp.zeros_like(o_ref)


def _experts(xb, block_e, n_used, w_gu, b_gu, w_down, b_down, layer):
    rows = xb.shape[0]
    nb = rows // MOE_ROWS

    def wmap(n, be, nu):
        return (layer, be[n], 0, 0)

    grid_spec = pltpu.PrefetchScalarGridSpec(
        num_scalar_prefetch=2,
        grid=(nb,),
        in_specs=[pl.BlockSpec((MOE_ROWS, D_MODEL), lambda n, be, nu: (n, 0)),
                  pl.BlockSpec((1, 1, D_MODEL, 2 * D_FF), wmap),
                  pl.BlockSpec((1, 1, 1, 2 * D_FF), wmap),
                  pl.BlockSpec((1, 1, D_FF, D_MODEL), wmap),
                  pl.BlockSpec((1, 1, 1, D_MODEL), wmap)],
        out_specs=pl.BlockSpec((MOE_ROWS, D_MODEL), lambda n, be, nu: (n, 0)),
        scratch_shapes=[pltpu.VMEM((D_MODEL, 2 * D_FF), BF16), pltpu.VMEM((D_FF, D_MODEL), BF16)])
    return pl.pallas_call(
        _expert_kernel,
        grid_spec=grid_spec,
        out_shape=jax.ShapeDtypeStruct((rows, D_MODEL), F32),
        compiler_params=_cparams(("arbitrary",)),
        name="experts",
    )(block_e, n_used, xb, w_gu, b_gu.reshape(DEPTH, N_EXPERTS, 1, 2 * D_FF), w_down,
      b_down.reshape(DEPTH, N_EXPERTS, 1, D_MODEL))


def _finish_kernel(x1_ref, yk_ref, tg_ref, g_ref, b_ref, wg_ref, pp_ref, ps_ref, wp_ref, o_ref, *, prompt_blocks):
    tg = tg_ref[...]
    moe = jnp.zeros(x1_ref.shape, F32)
    for k in range(TOP_K):
        moe = moe + tg[:, k:k + 1] * yk_ref[k]
    x2 = _layer_norm(DEEPNORM_ALPHA * x1_ref[...] + moe, g_ref[...], b_ref[...])
    gate = _sigmoid(jnp.dot(x2.astype(BF16), wg_ref[...], preferred_element_type=F32))
    p = jnp.where(pl.program_id(0) < prompt_blocks, pp_ref[0], ps_ref[0])
    proj = jnp.dot(p.astype(BF16), wp_ref[...], preferred_element_type=F32)
    o_ref[...] = x2 + gate * proj


def _finish(x1, yk, tg, g, b, w_pg, p_p, p_s, layer, w_pp, tm=512):
    m = x1.shape[0]
    n_p = p_p.shape[1] // tm
    n_s = p_s.shape[1] // tm
    row = lambda i: (i, 0)
    const = lambda i: (0, 0)
    return pl.pallas_call(
        functools.partial(_finish_kernel, prompt_blocks=n_p),
        grid=(m // tm,),
        in_specs=[pl.BlockSpec((tm, D_MODEL), row),
                  pl.BlockSpec((TOP_K, tm, D_MODEL), lambda i: (0, i, 0)),
                  pl.BlockSpec((tm, LANES), row),
                  pl.BlockSpec((1, D_MODEL), const),
                  pl.BlockSpec((1, D_MODEL), const),
                  pl.BlockSpec((D_MODEL, D_MODEL), const),
                  pl.BlockSpec((1, tm, PLE_DIM), lambda i: (layer, jnp.minimum(i, n_p - 1), 0)),
                  pl.BlockSpec((1, tm, PLE_DIM), lambda i: (layer, jnp.clip(i - n_p, 0, n_s - 1), 0)),
                  pl.BlockSpec((PLE_DIM, D_MODEL), const)],
        out_specs=pl.BlockSpec((tm, D_MODEL), row),
        out_shape=jax.ShapeDtypeStruct((m, D_MODEL), F32),
        compiler_params=_cparams(("parallel",)),
        name="finish",
    )(x1, yk, tg, g.reshape(1, D_MODEL), b.reshape(1, D_MODEL), w_pg, p_p, p_s, w_pp)


def _moe_and_finish(layer, x1, te, tg, p_p, p_s, ln2_g, ln2_b, w_gu, b_gu, w_down, b_down, w_pg, w_pp):
    t = x1.shape[0]
    a = t * TOP_K
    rank, cnt = _route_ranks(te)
    counts = cnt[0, :N_EXPERTS]
    padded = (counts + MOE_ROWS - 1) // MOE_ROWS * MOE_ROWS
    pend = jnp.cumsum(padded)
    pstart = pend - padded
    e4 = te[:, :TOP_K]
    dest = pstart[e4] + rank[:, :TOP_K]
    nb = a // MOE_ROWS + N_EXPERTS
    block_start = jnp.arange(nb, dtype=I32) * MOE_ROWS
    block_e = jnp.minimum(jnp.sum((pend[None, :] <= block_start[:, None]).astype(I32), axis=1), N_EXPERTS - 1)
    n_used = (pend[-1:] // MOE_ROWS).astype(I32)
    flat = jnp.arange(a, dtype=I32)
    tok_sorted = (jnp.sort(e4.reshape(-1) * a + flat) % a) // TOP_K
    start = jnp.cumsum(counts) - counts
    slot_e = jnp.repeat(block_e, MOE_ROWS)
    entry = jnp.arange(nb * MOE_ROWS, dtype=I32) - pstart[slot_e] + start[slot_e]
    slot_tok = tok_sorted[jnp.clip(entry, 0, a - 1)]
    xb = x1[slot_tok]
    yb = _experts(xb, block_e, n_used, w_gu, b_gu, w_down, b_down, layer)
    yk = yb[dest.T]
    return _finish(x1, yk, tg, ln2_g, ln2_b, w_pg, p_p, p_s, layer, w_pp)


def kernel(x_prompt, x_sample, cache_k, cache_v, cache_logf, state_ssm, state_conv, page_table, p_prompt, p_sample,
           fox_w_in, fox_b_f, fox_w_out, ssd_w_in, ssd_conv_w, ssd_conv_b, ssd_dt_bias, ssd_a_log, ssd_d,
           ssd_norm_w, ssd_w_out, ln1_g, ln1_b, ln2_g, ln2_b, moe_w_router, moe_b_router, moe_w_gu, moe_b_gu,
           moe_w_down, moe_b_down, ple_w_gate, ple_w_proj):
    bsz, seq, _ = x_prompt.shape
    bd, ld, _ = x_sample.shape
    tp = bsz * seq
    ts = bd * ld
    x = jnp.concatenate([x_prompt.reshape(tp, D_MODEL), x_sample.reshape(ts, D_MODEL)], axis=0)
    p_p = p_prompt.reshape(DEPTH, tp, PLE_DIM)
    p_s = p_sample.reshape(DEPTH, ts, PLE_DIM)

    prompt, sample = (0, tp), (tp, ts)
    kp, vp, lfp, hp, cp = [], [], [], [], []
    ks_, vs_, lfs, hs, cs = [], [], [], [], []
    for i in range(DEPTH):
        j = i // N_MIXERS
        if i % N_MIXERS == 0:
            w_in = fox_w_in[j]
            wq = w_in[:, :FOX_WIDTH].astype(BF16)
            wk = w_in[:, FOX_WIDTH:2 * FOX_WIDTH].astype(BF16)
            wv = w_in[:, 2 * FOX_WIDTH:3 * FOX_WIDTH].astype(BF16)
            wf = w_in[:, 3 * FOX_WIDTH:]
            kt_p = _mm_t(x, wk.T, bsz, seq)
            vt_p = _mm_t(x, wv.T, bsz, seq)
            k_s3 = _mm(x, wk, F32, sample).reshape(bd, ld, FOX_WIDTH)
            v_s3 = _mm(x, wv, F32, sample).reshape(bd, ld, FOX_WIDTH)
            lf_s3 = _gate_rows(x, wf, fox_b_f[j], "log_sigmoid", sample).reshape(bd, ld, FOX_HEADS)
            q_p = _mm(x, wq, BF16, prompt, scale=FOX_SCALE * LOG2E).reshape(bsz, seq, FOX_WIDTH)
            q_s = _mm(x, wq, F32, sample, scale=FOX_SCALE * LOG2E).reshape(bd, ld, FOX_WIDTH)
            lft_p, c4 = _gate_cols(x, bsz, seq, wf.T, fox_b_f[j], "log_sigmoid", True, LOG2E)
            c4 = c4.reshape(bsz, FOX_HEADS, 1, seq)
            o_p = _fox_attn(q_p, kt_p, vt_p, c4)
            o_s = _fox_decode(q_s, k_s3, v_s3, lf_s3, cache_k, cache_v, cache_logf, page_table, j)
            o_p, o_s = o_p.reshape(tp, FOX_WIDTH), o_s.reshape(ts, FOX_WIDTH)
            w_out = fox_w_out[j].astype(BF16)
            kp.append(kt_p.reshape(bsz, FOX_HEADS, FOX_HEAD_DIM, seq).transpose(0, 3, 1, 2))
            vp.append(vt_p.reshape(bsz, FOX_HEADS, FOX_HEAD_DIM, seq).transpose(0, 3, 1, 2))
            lfp.append(lft_p.transpose(0, 2, 1))
            ks_.append(k_s3.reshape(bd, ld, FOX_HEADS, FOX_HEAD_DIM))
            vs_.append(v_s3.reshape(bd, ld, FOX_HEADS, FOX_HEAD_DIM))
            lfs.append(lf_s3)
        else:
            w_in = ssd_w_in[j]
            wz = w_in[:, :D_INNER].astype(BF16)
            wx = w_in[:, D_INNER:D_INNER + CONV_DIM].astype(BF16)
            wd = w_in[:, D_INNER + CONV_DIM:]
            z_p3 = _mm(x, wz, F32, prompt).reshape(bsz, seq, D_INNER)
            z_s3 = _mm(x, wz, F32, sample).reshape(bd, ld, D_INNER)
            xbc_p3 = _mm(x, wx, F32, prompt).reshape(bsz, seq, CONV_DIM)
            xbc_s3 = _mm(x, wx, F32, sample).reshape(bd, ld, CONV_DIM)
            dt_p3 = _gate_rows(x, wd, ssd_dt_bias[j], "softplus", prompt).reshape(bsz, seq, SSD_HEADS)
            dt_s3 = _gate_rows(x, wd, ssd_dt_bias[j], "softplus", sample).reshape(bd, ld, SSD_HEADS)
            dtt = _gate_cols(x, bsz, seq, wd.T, ssd_dt_bias[j], "softplus", False)
            d_lanes = jnp.repeat(ssd_d[j], SSD_HEAD_DIM).reshape(1, D_INNER)
            y_p, h_p = _ssd_prompt(xbc_p3, z_p3, dt_p3, dtt,
                                   ssd_conv_w[j], ssd_conv_b[j], ssd_a_log[j], d_lanes, ssd_norm_w[j])
            y_s, h_s = _ssd_sample(xbc_s3, state_conv[j], z_s3, dt_s3, state_ssm, j,
                                   ssd_conv_w[j], ssd_conv_b[j], ssd_a_log[j], d_lanes, ssd_norm_w[j])
            o_p, o_s = y_p.reshape(tp, D_INNER), y_s.reshape(ts, D_INNER)
            w_out = ssd_w_out[j].astype(BF16)
            tail = CONV_K - 1
            hp.append(h_p)
            cp.append(xbc_p3[:, seq - tail:, :])
            hs.append(h_s)
            cs.append(jnp.concatenate([state_conv[j], xbc_s3], axis=1)[:, ld:, :])
        x1, te, tg = _post_mix(o_p, o_s, w_out, x, ln1_g[i], ln1_b[i], moe_w_router[i], moe_b_router[i])
        x = _moe_and_finish(i, x1, te, tg, p_p, p_s, ln2_g[i], ln2_b[i], moe_w_gu, moe_b_gu,
                            moe_w_down, moe_b_down, ple_w_gate[i].astype(BF16), ple_w_proj[i].astype(BF16))
    y_prompt = x[:tp].reshape(bsz, seq, D_MODEL)
    y_sample = x[tp:].reshape(bd, ld, D_MODEL)
    return (y_prompt, y_sample, jnp.stack(kp), jnp.stack(vp), jnp.stack(lfp), jnp.stack(hp), jnp.stack(cp),
            jnp.stack(ks_), jnp.stack(vs_), jnp.stack(lfs), jnp.stack(hs), jnp.stack(cs))
```

```python
import functools

import jax
import jax.numpy as jnp
from jax import lax
from jax.experimental import pallas as pl
from jax.experimental.pallas import tpu as pltpu

F32 = jnp.float32
BF16 = jnp.bfloat16
I32 = jnp.int32
HI = lax.Precision.HIGHEST

D_MODEL = 1024
DEPTH = 4
PAGE_SIZE = 128
N_MIXERS = 2
FOX_HEAD_DIM = 64
FOX_HEADS = D_MODEL // FOX_HEAD_DIM
FOX_WIDTH = FOX_HEADS * FOX_HEAD_DIM
FOX_SCALE = FOX_HEAD_DIM ** -0.5
D_INNER = 2 * D_MODEL
SSD_HEAD_DIM = 64
SSD_HEADS = D_INNER // SSD_HEAD_DIM
SSD_GROUPS = 8
SSD_HEADS_PER_GROUP = SSD_HEADS // SSD_GROUPS
D_STATE = 128
CONV_K = 4
GN = SSD_GROUPS * D_STATE
CONV_DIM = D_INNER + 2 * GN
SSD_CHUNK = 128
RMS_EPS = 1e-5
N_EXPERTS = 32
TOP_K = 4
D_FF = D_MODEL
SWIGLU_LIMIT = 7.0
GLU_ALPHA = 1.702
PLE_DIM = 256
LN_EPS = 1e-5
DEEPNORM_ALPHA = (2 * DEPTH) ** 0.25

LANES = 128
SUBLANES = 8
MOE_ROWS = 256
VMEM_LIMIT = 56 * 1024 * 1024
NEG_BIG = -1e30
LOG2E = 1.4426950408889634
DECODE_PAGES_PER_STEP = 16


def _cparams(sem):
    return pltpu.CompilerParams(dimension_semantics=sem, vmem_limit_bytes=VMEM_LIMIT)


def _iota(shape, dim):
    return lax.broadcasted_iota(I32, shape, dim)


def _log_sigmoid(z):
    return jnp.minimum(z, 0.0) - jnp.log1p(jnp.exp(-jnp.abs(z)))


def _softplus(z):
    return jnp.maximum(z, 0.0) + jnp.log1p(jnp.exp(-jnp.abs(z)))


def _sigmoid(z):
    return 1.0 / (1.0 + jnp.exp(-z))


_ACT = {"log_sigmoid": _log_sigmoid, "softplus": _softplus}


def _dot_nt(a, b, **kw):
    return lax.dot_general(a, b, (((1,), (1,)), ((), ())), preferred_element_type=F32, **kw)


def _dot_tn(a, b, **kw):
    return lax.dot_general(a, b, (((0,), (0,)), ((), ())), preferred_element_type=F32, **kw)


def _mm_kernel(x_ref, w_ref, o_ref, *, scale):
    acc = jnp.dot(x_ref[...].astype(BF16), w_ref[...], preferred_element_type=F32)
    if scale != 1.0:
        acc = acc * scale
    o_ref[...] = acc.astype(o_ref.dtype)


def _mm(x, w, out_dtype, rows, scale=1.0, tm=512, tn=1024):
    k = x.shape[1]
    n = w.shape[1]
    row0, m = rows
    tm, tn = min(tm, m), min(tn, n)
    blk0 = row0 // tm
    return pl.pallas_call(
        functools.partial(_mm_kernel, scale=scale),
        grid=(m // tm, n // tn),
        in_specs=[pl.BlockSpec((tm, k), lambda i, j: (i + blk0, 0)),
                  pl.BlockSpec((k, tn), lambda i, j: (0, j))],
        out_specs=pl.BlockSpec((tm, tn), lambda i, j: (i, j)),
        out_shape=jax.ShapeDtypeStruct((m, n), out_dtype),
        compiler_params=_cparams(("parallel", "parallel")),
        name="mm",
    )(x, w)


def _mm_t_kernel(x_ref, wt_ref, o_ref):
    o_ref[0] = _dot_nt(wt_ref[...], x_ref[...].astype(BF16))


def _mm_t(x, wt, bsz, seq, tm=512):
    k = x.shape[1]
    n = wt.shape[0]
    tm = min(tm, seq)
    per_seq = seq // tm
    return pl.pallas_call(
        _mm_t_kernel,
        grid=(bsz * per_seq,),
        in_specs=[pl.BlockSpec((tm, k), lambda i: (i, 0)),
                  pl.BlockSpec((n, k), lambda i: (0, 0))],
        out_specs=pl.BlockSpec((1, n, tm), lambda i: (i // per_seq, 0, i % per_seq)),
        out_shape=jax.ShapeDtypeStruct((bsz, n, seq), F32),
        compiler_params=_cparams(("parallel",)),
        name="mm_t",
    )(x, wt)


def _gate_rows_kernel(x_ref, w_ref, b_ref, o_ref, *, kind):
    z = jnp.dot(x_ref[...], w_ref[...], precision=HI, preferred_element_type=F32) + b_ref[...]
    o_ref[...] = _ACT[kind](z)


def _gate_rows(x, w, b, kind, rows, tm=512):
    k = x.shape[1]
    n = w.shape[1]
    row0, m = rows
    blk0 = row0 // tm
    return pl.pallas_call(
        functools.partial(_gate_rows_kernel, kind=kind),
        grid=(m // tm,),
        in_specs=[pl.BlockSpec((tm, k), lambda i: (i + blk0, 0)),
                  pl.BlockSpec((k, n), lambda i: (0, 0)),
                  pl.BlockSpec((1, n), lambda i: (0, 0))],
        out_specs=pl.BlockSpec((tm, n), lambda i: (i, 0)),
        out_shape=jax.ShapeDtypeStruct((m, n), F32),
        compiler_params=_cparams(("parallel",)),
        name="gate_rows",
    )(x, w, b.reshape(1, n))


def _gate_cols_kernel(x_ref, wt_ref, b_ref, *refs, kind, cumsum, cum_scale):
    z = _dot_nt(wt_ref[...], x_ref[...], precision=HI) + b_ref[...]
    y = _ACT[kind](z)
    refs[0][0] = y
    if cumsum:
        c_ref, carry_ref = refs[1], refs[2]

        @pl.when(pl.program_id(1) == 0)
        def _():
            carry_ref[...] = jnp.zeros_like(carry_ref)
        tl = y.shape[1]
        upper = (_iota((tl, tl), 0) <= _iota((tl, tl), 1)).astype(F32)
        cs = jnp.dot(y, upper, precision=HI, preferred_element_type=F32) + carry_ref[...]
        c_ref[0] = cs * cum_scale
        carry_ref[...] = cs[:, tl - 1:tl]


def _gate_cols(x, bsz, seq, wt, b, kind, cumsum, cum_scale=1.0, tl=512):
    k = x.shape[1]
    n = wt.shape[0]
    tl = min(tl, seq)
    per_seq = seq // tl
    n_out = 2 if cumsum else 1
    out_spec = pl.BlockSpec((1, n, tl), lambda bi, li: (bi, 0, li))
    outs = pl.pallas_call(
        functools.partial(_gate_cols_kernel, kind=kind, cumsum=cumsum, cum_scale=cum_scale),
        grid=(bsz, per_seq),
        in_specs=[pl.BlockSpec((tl, k), lambda bi, li: (bi * per_seq + li, 0)),
                  pl.BlockSpec((n, k), lambda bi, li: (0, 0)),
                  pl.BlockSpec((n, 1), lambda bi, li: (0, 0))],
        out_specs=[out_spec] * n_out,
        out_shape=[jax.ShapeDtypeStruct((bsz, n, seq), F32)] * n_out,
        scratch_shapes=[pltpu.VMEM((n, 1), F32)] if cumsum else [],
        compiler_params=_cparams(("parallel", "arbitrary")),
        name="gate_cols",
    )(x, wt, b.reshape(n, 1))
    return tuple(outs) if cumsum else outs[0]


def _fox_attn_kernel(q_ref, k_ref, v_ref, c_ref, o_ref, kb_ref, vb_ref, *, tq, tk):
    qi = pl.program_id(2)
    seq = k_ref.shape[2]

    @pl.when(qi == 0)
    def _():
        def cast_block(i, carry):
            cols = pl.ds(pl.multiple_of(i * tq, tq), tq)
            kb_ref[:, cols] = k_ref[0, :, cols].astype(BF16)
            vb_ref[cols, :] = v_ref[0, :, cols].T.astype(BF16)
            return carry
        lax.fori_loop(0, seq // tq, cast_block, 0)

    q = q_ref[0]
    lane = _iota((tq, LANES), 1)
    first = lane < FOX_HEAD_DIM
    zero = jnp.zeros_like(q)
    qh = (jnp.where(first, q, zero), jnp.where(first, zero, q))

    def step(kb, carry, width, masked):
        rows = pl.ds(pl.multiple_of(kb * width, width), width)
        kblk = kb_ref[:, rows]
        vblk = vb_ref[rows, :]
        if masked:
            visible = (_iota((tq, width), 0) - _iota((tq, width), 1)) >= kb * width - qi * tq
        new = []
        for hh in range(2):
            m, l, acc = carry[hh]
            s = jnp.dot(qh[hh], kblk, preferred_element_type=F32) - c_ref[0, hh, :, rows]
            if masked:
                s = jnp.where(visible, s, -jnp.inf)
            m_new = jnp.maximum(m, jnp.max(s, axis=1, keepdims=True))
            alpha = jnp.exp2(m - m_new)
            p = jnp.exp2(s - m_new)
            l = alpha * l + jnp.sum(p, axis=1, keepdims=True)
            acc = alpha * acc + jnp.dot(p.astype(BF16), vblk, preferred_element_type=F32)
            new.append((m_new, l, acc))
        return tuple(new)

    init = (jnp.full((tq, 1), -jnp.inf, F32), jnp.zeros((tq, 1), F32), jnp.zeros((tq, LANES), F32))
    n_full = (qi * tq) // tk
    carry = lax.fori_loop(0, n_full, functools.partial(step, width=tk, masked=False), (init, init))
    carry = lax.fori_loop(n_full * (tk // tq), qi, functools.partial(step, width=tq, masked=False), carry)
    carry = step(qi, carry, tq, True)
    outs = [acc / l for _, l, acc in carry]
    o_ref[0] = jnp.where(first, outs[0], outs[1]).astype(o_ref.dtype)


def _fox_attn(q3, kt3, vt3, c4, tq=512, tk=1024):
    bsz, seq, width = q3.shape
    tq = min(tq, seq)
    tk = min(tk, seq)
    pairs = width // LANES
    return pl.pallas_call(
        functools.partial(_fox_attn_kernel, tq=tq, tk=tk),
        grid=(bsz, pairs, seq // tq),
        in_specs=[pl.BlockSpec((1, tq, LANES), lambda b, j, i: (b, i, j)),
                  pl.BlockSpec((1, LANES, seq), lambda b, j, i: (b, j, 0)),
                  pl.BlockSpec((1, LANES, seq), lambda b, j, i: (b, j, 0)),
                  pl.BlockSpec((1, 2, 1, seq), lambda b, j, i: (b, j, 0, 0))],
        out_specs=pl.BlockSpec((1, tq, LANES), lambda b, j, i: (b, i, j)),
        out_shape=jax.ShapeDtypeStruct((bsz, seq, width), BF16),
        scratch_shapes=[pltpu.VMEM((LANES, seq), BF16), pltpu.VMEM((seq, LANES), BF16)],
        compiler_params=_cparams(("parallel", "parallel", "arbitrary")),
        name="fox_attn",
    )(q3, kt3, vt3, c4)


def _fox_dec_kernel(*refs, pages_per_step):
    n = pages_per_step
    q_ref = refs[1]
    kc_refs = refs[2:2 + n]
    vc_refs = refs[2 + n:2 + 2 * n]
    lc_refs = refs[2 + 2 * n:2 + 3 * n]
    (kn_ref, vn_ref, ln_ref, o_ref,
     qbd_ref, kpad_ref, vpad_ref, m_ref, l_ref, acc_ref, carry_ref) = refs[2 + 3 * n:]
    step = pl.program_id(1)
    ld = q_ref.shape[1]
    rows = FOX_HEADS * ld
    width = q_ref.shape[2]

    @pl.when(step == 0)
    def _():
        q = q_ref[0]
        qt = jnp.broadcast_to(q[None], (FOX_HEADS, ld, width)).reshape(rows, width)
        head_of_row = _iota((rows, width), 0) // ld
        head_of_col = _iota((rows, width), 1) // FOX_HEAD_DIM
        qbd_ref[...] = jnp.where(head_of_row == head_of_col, qt, 0.0).astype(BF16)
        m_ref[...] = jnp.full(m_ref.shape, -jnp.inf, F32)
        l_ref[...] = jnp.zeros_like(l_ref)
        acc_ref[...] = jnp.zeros_like(acc_ref)
        carry_ref[...] = jnp.zeros_like(carry_ref)

    def attend(scores, lf_ts, pvs, causal):
        upper = (_iota((PAGE_SIZE, PAGE_SIZE), 0) <= _iota((PAGE_SIZE, PAGE_SIZE), 1)).astype(F32)
        offset = carry_ref[...]
        biased = []
        for s, lf_t in zip(scores, lf_ts):
            c = jnp.dot(lf_t, upper, precision=HI, preferred_element_type=F32)
            c_rows = jnp.broadcast_to(((c + offset) * LOG2E)[:, None, :], (FOX_HEADS, ld, PAGE_SIZE))
            biased.append(s - c_rows.reshape(rows, PAGE_SIZE))
            offset = offset + c[:, PAGE_SIZE - 1:PAGE_SIZE]
        carry_ref[...] = offset
        s = jnp.concatenate(biased, axis=1) if len(biased) > 1 else biased[0]
        if causal:
            tok = _iota(s.shape, 0) % ld
            key = _iota(s.shape, 1)
            s = jnp.where(key <= tok, s, -jnp.inf)
        m_old = m_ref[...]
        m_new = jnp.maximum(m_old, jnp.max(s, axis=1, keepdims=True))
        alpha = jnp.exp2(m_old - m_new)
        pr = jnp.exp2(s - m_new)
        l_ref[...] = alpha * l_ref[...] + jnp.sum(pr, axis=1, keepdims=True)
        pb = pr.astype(BF16)
        pv = pvs[0](pb[:, 0:PAGE_SIZE])
        for i in range(1, len(pvs)):
            pv = pv + pvs[i](pb[:, i * PAGE_SIZE:(i + 1) * PAGE_SIZE])
        acc_ref[...] = alpha * acc_ref[...] + pv
        m_ref[...] = m_new

    q = qbd_ref[...]
    scores = [jnp.dot(q, kc_refs[i][0, 0].astype(BF16), preferred_element_type=F32) for i in range(n)]
    pvs = [lambda p, i=i: _dot_nt(p, vc_refs[i][0, 0].astype(BF16)) for i in range(n)]
    attend(scores, [lc_refs[i][0, 0] for i in range(n)], pvs, False)

    @pl.when(step == pl.num_programs(1) - 1)
    def _():
        kpad_ref[...] = jnp.zeros_like(kpad_ref)
        vpad_ref[...] = jnp.zeros_like(vpad_ref)
        kpad_ref[0:ld, :] = kn_ref[0]
        vpad_ref[0:ld, :] = vn_ref[0]
        s = _dot_nt(qbd_ref[...], kpad_ref[...].astype(BF16))
        attend([s], [ln_ref[0]], [lambda p: jnp.dot(p, vpad_ref[...].astype(BF16), preferred_element_type=F32)], True)
        a3 = (acc_ref[...] / l_ref[...]).reshape(FOX_HEADS, ld, width)
        own = _iota((FOX_HEADS, ld, width), 0) == _iota((FOX_HEADS, ld, width), 2) // FOX_HEAD_DIM
        o_ref[0] = jnp.sum(jnp.where(own, a3, 0.0), axis=0)


def _fox_decode(q3, kn3, vn3, ln3, cache_k, cache_v, cache_lf, page_table, layer):
    bd, ld, width = q3.shape
    n_fox, n_pool = cache_k.shape[:2]
    n_pages = page_table.shape[1]
    rows = FOX_HEADS * ld
    pps = DECODE_PAGES_PER_STEP if n_pages % DECODE_PAGES_PER_STEP == 0 else 1
    n_steps = n_pages // pps
    kc_t = cache_k.transpose(0, 1, 3, 4, 2).reshape(n_fox, n_pool, width, PAGE_SIZE)
    vc_t = cache_v.transpose(0, 1, 3, 4, 2).reshape(n_fox, n_pool, width, PAGE_SIZE)
    lc_t = cache_lf.transpose(0, 1, 3, 2)
    ln_t = jnp.zeros((bd, FOX_HEADS, PAGE_SIZE), F32).at[:, :, :ld].set(ln3.transpose(0, 2, 1))

    def cache_map(i):
        return lambda b, s, pt: (layer, pt[b, s * pps + i], 0, 0)

    def seq_map(b, s, pt):
        return (b, 0, 0)

    grid_spec = pltpu.PrefetchScalarGridSpec(
        num_scalar_prefetch=1,
        grid=(bd, n_steps),
        in_specs=[pl.BlockSpec((1, ld, width), seq_map)]
        + [pl.BlockSpec((1, 1, width, PAGE_SIZE), cache_map(i)) for i in range(pps)]
        + [pl.BlockSpec((1, 1, width, PAGE_SIZE), cache_map(i)) for i in range(pps)]
        + [pl.BlockSpec((1, 1, FOX_HEADS, PAGE_SIZE), cache_map(i)) for i in range(pps)]
        + [pl.BlockSpec((1, ld, width), seq_map),
           pl.BlockSpec((1, ld, width), seq_map),
           pl.BlockSpec((1, FOX_HEADS, PAGE_SIZE), seq_map)],
        out_specs=pl.BlockSpec((1, ld, width), seq_map),
        scratch_shapes=[pltpu.VMEM((rows, width), BF16),
                        pltpu.VMEM((PAGE_SIZE, width), F32),
                        pltpu.VMEM((PAGE_SIZE, width), F32),
                        pltpu.VMEM((rows, 1), F32),
                        pltpu.VMEM((rows, 1), F32),
                        pltpu.VMEM((rows, width), F32),
                        pltpu.VMEM((FOX_HEADS, 1), F32)])
    return pl.pallas_call(
        functools.partial(_fox_dec_kernel, pages_per_step=pps),
        grid_spec=grid_spec,
        out_shape=jax.ShapeDtypeStruct((bd, ld, width), F32),
        compiler_params=_cparams(("parallel", "arbitrary")),
        name="fox_decode",
    )(page_table, q3, *([kc_t] * pps), *([vc_t] * pps), *([lc_t] * pps), kn3, vn3, ln_t)


def _ssd_chunk_kernel(xbc_ref, z_ref, dt_ref, dtt_ref, cw_ref, cb_ref, al_ref, alt_ref, dsk_ref, nw_ref,
                      y_ref, st_ref, xp_ref, act_ref, ysc_ref, state_ref):
    c = pl.program_id(1)
    nc = pl.num_programs(1)
    q = SSD_CHUNK
    pair_w = 2 * SSD_HEAD_DIM

    @pl.when(c == 0)
    def _():
        state_ref[...] = jnp.zeros_like(state_ref)
        xp_ref[0:SUBLANES, :] = jnp.zeros((SUBLANES, CONV_DIM), F32)

    xp_ref[SUBLANES:SUBLANES + q, :] = xbc_ref[0]
    conv = cb_ref[...]
    for t in range(CONV_K):
        lo = SUBLANES - (CONV_K - 1) + t
        conv = conv + xp_ref[lo:lo + q, :] * cw_ref[t:t + 1, :]
    xp_ref[0:SUBLANES, :] = xp_ref[q:q + SUBLANES, :]
    act_ref[...] = conv * _sigmoid(conv)

    dt = dt_ref[0]
    dtt = dtt_ref[0]
    a = dt * (-jnp.exp(al_ref[...]))
    at = dtt * (-jnp.exp(alt_ref[...]))
    lower = (_iota((q, q), 0) >= _iota((q, q), 1))
    acum = jnp.dot(lower.astype(F32), a, precision=HI, preferred_element_type=F32)
    acum_t = jnp.dot(at, (_iota((q, q), 0) <= _iota((q, q), 1)).astype(F32), precision=HI,
                     preferred_element_type=F32)
    a_end = acum[q - 1:q, :]
    decay_end = jnp.exp(a_end - acum) * dt
    exp_acum = jnp.exp(acum)
    chunk_decay = jnp.exp(acum_t[:, q - 1:q])

    lane = _iota((q, pair_w), 1)
    first = lane < SSD_HEAD_DIM
    row_first = _iota((pair_w, 1), 0) < SSD_HEAD_DIM
    for g in range(SSD_GROUPS):
        bg = act_ref[:, D_INNER + g * D_STATE:D_INNER + (g + 1) * D_STATE].astype(BF16)
        cg = act_ref[:, D_INNER + GN + g * D_STATE:D_INNER + GN + (g + 1) * D_STATE].astype(BF16)
        cb = _dot_nt(cg, bg)
        for pr in range(SSD_HEADS_PER_GROUP // 2):
            pi = g * (SSD_HEADS_PER_GROUP // 2) + pr
            h0 = 2 * pi
            xpair = act_ref[:, pi * pair_w:(pi + 1) * pair_w]
            xpair_b = xpair.astype(BF16)
            ys = []
            for hh in range(2):
                h = h0 + hh
                seg = acum[:, h:h + 1] - acum_t[h:h + 1, :]
                w = cb * jnp.exp(jnp.where(lower, seg, -jnp.inf)) * dtt[h:h + 1, :]
                ys.append(jnp.dot(w.astype(BF16), xpair_b, preferred_element_type=F32))
            y_diag = jnp.where(first, ys[0], ys[1])
            s_pair = state_ref[pi]
            ea = jnp.where(first, exp_acum[:, h0:h0 + 1], exp_acum[:, h0 + 1:h0 + 2])
            y_off = _dot_nt(cg, s_pair.astype(BF16)) * ea
            ysc_ref[:, pi * pair_w:(pi + 1) * pair_w] = y_diag + y_off
            de = jnp.where(first, decay_end[:, h0:h0 + 1], decay_end[:, h0 + 1:h0 + 2])
            contrib = _dot_tn((xpair * de).astype(BF16), bg)
            cd = jnp.where(row_first, chunk_decay[h0:h0 + 1, :], chunk_decay[h0 + 1:h0 + 2, :])
            state_ref[pi] = s_pair * cd + contrib

    gw = D_INNER // SSD_GROUPS
    for g in range(SSD_GROUPS):
        cols = slice(g * gw, (g + 1) * gw)
        zg = z_ref[0, :, cols]
        yg = (ysc_ref[:, cols] + dsk_ref[:, cols] * act_ref[:, cols]) * (zg * _sigmoid(zg))
        ms = jnp.mean(yg * yg, axis=1, keepdims=True)
        y_ref[0, :, cols] = (yg * lax.rsqrt(ms + RMS_EPS) * nw_ref[:, cols]).astype(y_ref.dtype)

    @pl.when(c == nc - 1)
    def _():
        st_ref[0] = state_ref[...]


def _ssd_prompt(xbc3, z3, dt3, dtt3, conv_w, conv_b, a_log, d_lanes, norm_w):
    bsz, seq, _ = xbc3.shape
    q = SSD_CHUNK
    n_pairs = SSD_HEADS // 2
    const2 = lambda b, c: (0, 0)
    y, st = pl.pallas_call(
        _ssd_chunk_kernel,
        grid=(bsz, seq // q),
        in_specs=[pl.BlockSpec((1, q, CONV_DIM), lambda b, c: (b, c, 0)),
                  pl.BlockSpec((1, q, D_INNER), lambda b, c: (b, c, 0)),
                  pl.BlockSpec((1, q, SSD_HEADS), lambda b, c: (b, c, 0)),
                  pl.BlockSpec((1, SSD_HEADS, q), lambda b, c: (b, 0, c)),
                  pl.BlockSpec((CONV_K, CONV_DIM), const2),
                  pl.BlockSpec((1, CONV_DIM), const2),
                  pl.BlockSpec((1, SSD_HEADS), const2),
                  pl.BlockSpec((SSD_HEADS, 1), const2),
                  pl.BlockSpec((1, D_INNER), const2),
                  pl.BlockSpec((1, D_INNER), const2)],
        out_specs=[pl.BlockSpec((1, q, D_INNER), lambda b, c: (b, c, 0)),
                   pl.BlockSpec((1, n_pairs, 2 * SSD_HEAD_DIM, D_STATE), lambda b, c: (b, 0, 0, 0))],
        out_shape=[jax.ShapeDtypeStruct((bsz, seq, D_INNER), BF16),
                   jax.ShapeDtypeStruct((bsz, n_pairs, 2 * SSD_HEAD_DIM, D_STATE), F32)],
        scratch_shapes=[pltpu.VMEM((q + SUBLANES, CONV_DIM), F32),
                        pltpu.VMEM((q, CONV_DIM), F32),
                        pltpu.VMEM((q, D_INNER), F32),
                        pltpu.VMEM((n_pairs, 2 * SSD_HEAD_DIM, D_STATE), F32)],
        compiler_params=_cparams(("parallel", "arbitrary")),
        name="ssd_prompt",
    )(xbc3, z3, dt3, dtt3, conv_w, conv_b.reshape(1, CONV_DIM), a_log.reshape(1, SSD_HEADS),
      a_log.reshape(SSD_HEADS, 1), d_lanes, norm_w.reshape(1, D_INNER))
    return y, st.reshape(bsz, SSD_HEADS, SSD_HEAD_DIM, D_STATE)


def _ssd_step_kernel(xbc_ref, c0_ref, z_ref, dt_ref, s0_ref, cw_ref, cb_ref, al_ref, dsk_ref, nw_ref,
                     hexp_ref, gsum_ref, y_ref, s1_ref, xp_ref, xd_ref, bpad_ref):
    ld = xbc_ref.shape[1]
    tail = CONV_K - 1
    pair_w = 2 * SSD_HEAD_DIM
    gw = D_INNER // SSD_GROUPS

    xp_ref[SUBLANES - tail:SUBLANES, :] = c0_ref[0]
    xp_ref[SUBLANES:SUBLANES + ld, :] = xbc_ref[0]
    conv = cb_ref[...]
    for t in range(CONV_K):
        lo = SUBLANES - tail + t
        conv = conv + xp_ref[lo:lo + ld, :] * cw_ref[t:t + 1, :]
    act = conv * _sigmoid(conv)
    xs = act[:, :D_INNER]
    bm = act[:, D_INNER:D_INNER + GN]
    cm = act[:, D_INNER + GN:]

    dt = dt_ref[0]
    a = dt * (-jnp.exp(al_ref[...]))
    row = _iota((ld, SSD_HEADS), 0)
    acum = a
    sh = 1
    while sh < ld:
        acum = acum + jnp.where(row >= sh, pltpu.roll(acum, sh, 0), 0.0)
        sh *= 2
    a_end = acum[ld - 1:ld, :]
    hexp = hexp_ref[...]

    prod = jnp.concatenate([cm * bm[s:s + 1, :] for s in range(ld)], axis=0)
    cbh = jnp.dot(prod, gsum_ref[...], precision=HI, preferred_element_type=F32)
    a_l = jnp.concatenate([acum] * ld, axis=0)
    a_s = jnp.concatenate([jnp.broadcast_to(acum[s:s + 1, :], (ld, SSD_HEADS)) for s in range(ld)], axis=0)
    dt_s = jnp.concatenate([jnp.broadcast_to(dt[s:s + 1, :], (ld, SSD_HEADS)) for s in range(ld)], axis=0)
    pr_row = _iota((ld * ld, SSD_HEADS), 0)
    causal = (pr_row % ld) >= (pr_row // ld)
    w = cbh * jnp.exp(jnp.where(causal, a_l - a_s, -jnp.inf)) * dt_s
    wexp = jnp.dot(w, hexp, precision=HI, preferred_element_type=F32)
    y = jnp.zeros((ld, D_INNER), F32)
    for s in range(ld):
        y = y + wexp[s * ld:(s + 1) * ld, :] * xs[s:s + 1, :]

    ea = jnp.dot(jnp.exp(acum), hexp, precision=HI, preferred_element_type=F32)
    de = jnp.dot(jnp.exp(a_end - acum) * dt, hexp, precision=HI, preferred_element_type=F32)
    xd_ref[...] = jnp.zeros_like(xd_ref)
    bpad_ref[...] = jnp.zeros_like(bpad_ref)
    xd_ref[0:ld, :] = xs * de
    bpad_ref[0:ld, :] = bm
    chunk_decay = jnp.exp(a_end)
    pairs_per_group = SSD_HEADS_PER_GROUP // 2
    y_off = []
    for g in range(SSD_GROUPS):
        cg = cm[:, g * D_STATE:(g + 1) * D_STATE].astype(BF16)
        bg = bpad_ref[:, g * D_STATE:(g + 1) * D_STATE].astype(BF16)
        for pr in range(pairs_per_group):
            pi = g * pairs_per_group + pr
            s_pair = s0_ref[0, pi]
            y_off.append(_dot_nt(cg, s_pair.astype(BF16)))
            contrib = _dot_tn(xd_ref[:, pi * pair_w:(pi + 1) * pair_w].astype(BF16), bg)
            for hh in range(2):
                h = 2 * pi + hh
                rows = slice(hh * SSD_HEAD_DIM, (hh + 1) * SSD_HEAD_DIM)
                s1_ref[0, pi, rows, :] = s_pair[rows, :] * chunk_decay[0, h] + contrib[rows, :]
    y = y + jnp.concatenate(y_off, axis=1) * ea

    y = (y + dsk_ref[...] * xs) * (z_ref[0] * _sigmoid(z_ref[0]))
    for g in range(SSD_GROUPS):
        cols = slice(g * gw, (g + 1) * gw)
        yg = y[:, cols]
        ms = jnp.mean(yg * yg, axis=1, keepdims=True)
        y_ref[0, :, cols] = yg * lax.rsqrt(ms + RMS_EPS) * nw_ref[:, cols]


def _ssd_sample(xbc3, conv0, z3, dt3, state_all, layer, conv_w, conv_b, a_log, d_lanes, norm_w):
    bd, ld, _ = xbc3.shape
    n_pairs = SSD_HEADS // 2
    pair_w = 2 * SSD_HEAD_DIM
    hexp = (jnp.arange(D_INNER, dtype=I32)[None, :] // SSD_HEAD_DIM == jnp.arange(SSD_HEADS, dtype=I32)[:, None]).astype(F32)
    gsum = (jnp.arange(GN, dtype=I32)[:, None] // D_STATE ==
            jnp.arange(SSD_HEADS, dtype=I32)[None, :] // SSD_HEADS_PER_GROUP).astype(F32)
    s0 = state_all.reshape(state_all.shape[0] * bd, n_pairs, pair_w, D_STATE)
    const2 = lambda b: (0, 0)
    y, s1 = pl.pallas_call(
        _ssd_step_kernel,
        grid=(bd,),
        in_specs=[pl.BlockSpec((1, ld, CONV_DIM), lambda b: (b, 0, 0)),
                  pl.BlockSpec((1, CONV_K - 1, CONV_DIM), lambda b: (b, 0, 0)),
                  pl.BlockSpec((1, ld, D_INNER), lambda b: (b, 0, 0)),
                  pl.BlockSpec((1, ld, SSD_HEADS), lambda b: (b, 0, 0)),
                  pl.BlockSpec((1, n_pairs, pair_w, D_STATE), lambda b: (layer * bd + b, 0, 0, 0)),
                  pl.BlockSpec((CONV_K, CONV_DIM), const2),
                  pl.BlockSpec((1, CONV_DIM), const2),
                  pl.BlockSpec((1, SSD_HEADS), const2),
                  pl.BlockSpec((1, D_INNER), const2),
                  pl.BlockSpec((1, D_INNER), const2),
                  pl.BlockSpec((SSD_HEADS, D_INNER), const2),
                  pl.BlockSpec((GN, SSD_HEADS), const2)],
        out_specs=[pl.BlockSpec((1, ld, D_INNER), lambda b: (b, 0, 0)),
                   pl.BlockSpec((1, n_pairs, pair_w, D_STATE), lambda b: (b, 0, 0, 0))],
        out_shape=[jax.ShapeDtypeStruct((bd, ld, D_INNER), F32),
                   jax.ShapeDtypeStruct((bd, n_pairs, pair_w, D_STATE), F32)],
        scratch_shapes=[pltpu.VMEM((SUBLANES + ld, CONV_DIM), F32),
                        pltpu.VMEM((LANES, D_INNER), F32),
                        pltpu.VMEM((LANES, GN), F32)],
        compiler_params=_cparams(("parallel",)),
        name="ssd_sample",
    )(xbc3, conv0, z3, dt3, s0, conv_w, conv_b.reshape(1, CONV_DIM), a_log.reshape(1, SSD_HEADS),
      d_lanes, norm_w.reshape(1, D_INNER), hexp, gsum)
    return y, s1.reshape(bd, SSD_HEADS, SSD_HEAD_DIM, D_STATE)


def _layer_norm(y, g, b):
    mu = jnp.mean(y, axis=1, keepdims=True)
    d = y - mu
    var = jnp.mean(d * d, axis=1, keepdims=True)
    return d * lax.rsqrt(var + LN_EPS) * g + b


def _post_mix_kernel(op_ref, os_ref, w_ref, x_ref, g_ref, b_ref, wr_ref, br_ref, x1_ref, te_ref, tg_ref, *,
                     prompt_blocks):
    o = jnp.where(pl.program_id(0) < prompt_blocks, op_ref[...].astype(BF16), os_ref[...].astype(BF16))
    mix = jnp.dot(o, w_ref[...], preferred_element_type=F32)
    x1 = _layer_norm(DEEPNORM_ALPHA * x_ref[...] + mix, g_ref[...], b_ref[...])
    x1_ref[...] = x1
    logits = jnp.dot(x1, wr_ref[...], precision=HI, preferred_element_type=F32) + br_ref[...]
    lane = _iota(logits.shape, 1)
    vals, idxs = [], []
    cur = logits
    for _ in range(TOP_K):
        mx = jnp.max(cur, axis=1, keepdims=True)
        idx = jnp.min(jnp.where(cur == mx, lane, LANES), axis=1, keepdims=True)
        vals.append(mx)
        idxs.append(idx)
        cur = jnp.where(lane == idx, -jnp.inf, cur)
    ex = [jnp.exp(v - vals[0]) for v in vals]
    den = ex[0] + ex[1] + ex[2] + ex[3]
    te = jnp.zeros(logits.shape, I32)
    tg = jnp.zeros(logits.shape, F32)
    for k in range(TOP_K):
        te = jnp.where(lane == k, idxs[k], te)
        tg = jnp.where(lane == k, ex[k] / den, tg)
    te_ref[...] = te
    tg_ref[...] = tg


def _post_mix(o_p, o_s, w_out, x, g, b, w_router, b_router, tm=512):
    k = o_p.shape[1]
    m = x.shape[0]
    n_p = o_p.shape[0] // tm
    n_s = o_s.shape[0] // tm
    wr = jnp.zeros((D_MODEL, LANES), F32).at[:, :N_EXPERTS].set(w_router)
    br = jnp.full((1, LANES), NEG_BIG, F32).at[0, :N_EXPERTS].set(b_router)
    row = lambda i: (i, 0)
    const = lambda i: (0, 0)
    return pl.pallas_call(
        functools.partial(_post_mix_kernel, prompt_blocks=n_p),
        grid=(m // tm,),
        in_specs=[pl.BlockSpec((tm, k), lambda i: (jnp.minimum(i, n_p - 1), 0)),
                  pl.BlockSpec((tm, k), lambda i: (jnp.clip(i - n_p, 0, n_s - 1), 0)),
                  pl.BlockSpec((k, D_MODEL), const),
                  pl.BlockSpec((tm, D_MODEL), row),
                  pl.BlockSpec((1, D_MODEL), const),
                  pl.BlockSpec((1, D_MODEL), const),
                  pl.BlockSpec((D_MODEL, LANES), const),
                  pl.BlockSpec((1, LANES), const)],
        out_specs=[pl.BlockSpec((tm, D_MODEL), row),
                   pl.BlockSpec((tm, LANES), row), pl.BlockSpec((tm, LANES), row)],
        out_shape=[jax.ShapeDtypeStruct((m, D_MODEL), F32),
                   jax.ShapeDtypeStruct((m, LANES), I32), jax.ShapeDtypeStruct((m, LANES), F32)],
        compiler_params=_cparams(("parallel",)),
        name="post_mix",
    )(o_p, o_s, w_out, x, g.reshape(1, D_MODEL), b.reshape(1, D_MODEL), wr, br)


def _rank_kernel(te_ref, rank_ref, cnt_ref, carry_ref):
    @pl.when(pl.program_id(0) == 0)
    def _():
        carry_ref[...] = jnp.zeros_like(carry_ref)
    te = te_ref[...]
    tr = te.shape[0]
    lane = _iota((tr, LANES), 1)
    onehot = jnp.zeros((tr, LANES), F32)
    for k in range(TOP_K):
        onehot = onehot + (lane == te[:, k:k + 1]).astype(F32)
    strict = (_iota((tr, tr), 0) > _iota((tr, tr), 1)).astype(BF16)
    before = jnp.dot(strict, onehot.astype(BF16), preferred_element_type=F32) + carry_ref[0:1, :]
    out = jnp.zeros((tr, LANES), F32)
    for k in range(TOP_K):
        rk = jnp.sum(jnp.where(lane == te[:, k:k + 1], before, 0.0), axis=1, keepdims=True)
        out = jnp.where(lane == k, rk, out)
    rank_ref[...] = out.astype(I32)
    total = carry_ref[0:1, :] + jnp.sum(onehot, axis=0, keepdims=True)
    carry_ref[...] = jnp.broadcast_to(total, carry_ref.shape)
    cnt_ref[...] = jnp.broadcast_to(total, cnt_ref.shape).astype(I32)


def _route_ranks(te, tr=512):
    m = te.shape[0]
    return pl.pallas_call(
        _rank_kernel,
        grid=(m // tr,),
        in_specs=[pl.BlockSpec((tr, LANES), lambda i: (i, 0))],
        out_specs=[pl.BlockSpec((tr, LANES), lambda i: (i, 0)),
                   pl.BlockSpec((SUBLANES, LANES), lambda i: (0, 0))],
        out_shape=[jax.ShapeDtypeStruct((m, LANES), I32), jax.ShapeDtypeStruct((SUBLANES, LANES), I32)],
        scratch_shapes=[pltpu.VMEM((SUBLANES, LANES), F32)],
        compiler_params=_cparams(("arbitrary",)),
        name="route_ranks",
    )(te)


def _expert_kernel(be_ref, nu_ref, x_ref, wgu_ref, bgu_ref, wd_ref, bd_ref, o_ref, wgu_b, wd_b):
    n = pl.program_id(0)
    prev = be_ref[jnp.maximum(n - 1, 0)]
    used = n < nu_ref[0]

    @pl.when(jnp.logical_and(used, jnp.logical_or(n == 0, be_ref[n] != prev)))
    def _():
        wgu_b[...] = wgu_ref[0, 0].astype(BF16)
        wd_b[...] = wd_ref[0, 0].astype(BF16)

    @pl.when(used)
    def _():
        h = jnp.dot(x_ref[...].astype(BF16), wgu_b[...], preferred_element_type=F32) + bgu_ref[0, 0]
        g = jnp.minimum(h[:, :D_FF], SWIGLU_LIMIT)
        u = jnp.clip(h[:, D_FF:], -SWIGLU_LIMIT, SWIGLU_LIMIT)
        act = g * _sigmoid(GLU_ALPHA * g) * (u + 1.0)
        o_ref[...] = jnp.dot(act.astype(BF16), wd_b[...], preferred_element_type=F32) + bd_ref[0, 0]

    @pl.when(jnp.logical_not(used))
    def _():
        o_ref[...] = jnp.zeros_like(o_ref)


def _experts(xb, block_e, n_used, w_gu, b_gu, w_down, b_down, layer):
    rows = xb.shape[0]
    nb = rows // MOE_ROWS

    def wmap(n, be, nu):
        return (layer, be[n], 0, 0)

    grid_spec = pltpu.PrefetchScalarGridSpec(
        num_scalar_prefetch=2,
        grid=(nb,),
        in_specs=[pl.BlockSpec((MOE_ROWS, D_MODEL), lambda n, be, nu: (n, 0)),
                  pl.BlockSpec((1, 1, D_MODEL, 2 * D_FF), wmap),
                  pl.BlockSpec((1, 1, 1, 2 * D_FF), wmap),
                  pl.BlockSpec((1, 1, D_FF, D_MODEL), wmap),
                  pl.BlockSpec((1, 1, 1, D_MODEL), wmap)],
        out_specs=pl.BlockSpec((MOE_ROWS, D_MODEL), lambda n, be, nu: (n, 0)),
        scratch_shapes=[pltpu.VMEM((D_MODEL, 2 * D_FF), BF16), pltpu.VMEM((D_FF, D_MODEL), BF16)])
    return pl.pallas_call(
        _expert_kernel,
        grid_spec=grid_spec,
        out_shape=jax.ShapeDtypeStruct((rows, D_MODEL), F32),
        compiler_params=_cparams(("arbitrary",)),
        name="experts",
    )(block_e, n_used, xb, w_gu, b_gu.reshape(DEPTH, N_EXPERTS, 1, 2 * D_FF), w_down,
      b_down.reshape(DEPTH, N_EXPERTS, 1, D_MODEL))


def _finish_kernel(x1_ref, yk_ref, tg_ref, g_ref, b_ref, wg_ref, pp_ref, ps_ref, wp_ref, o_ref, *, prompt_blocks):
    tg = tg_ref[...]
    moe = jnp.zeros(x1_ref.shape, F32)
    for k in range(TOP_K):
        moe = moe + tg[:, k:k + 1] * yk_ref[k]
    x2 = _layer_norm(DEEPNORM_ALPHA * x1_ref[...] + moe, g_ref[...], b_ref[...])
    gate = _sigmoid(jnp.dot(x2.astype(BF16), wg_ref[...], preferred_element_type=F32))
    p = jnp.where(pl.program_id(0) < prompt_blocks, pp_ref[0], ps_ref[0])
    proj = jnp.dot(p.astype(BF16), wp_ref[...], preferred_element_type=F32)
    o_ref[...] = x2 + gate * proj


def _finish(x1, yk, tg, g, b, w_pg, p_p, p_s, layer, w_pp, tm=512):
    m = x1.shape[0]
    n_p = p_p.shape[1] // tm
    n_s = p_s.shape[1] // tm
    row = lambda i: (i, 0)
    const = lambda i: (0, 0)
    return pl.pallas_call(
        functools.partial(_finish_kernel, prompt_blocks=n_p),
        grid=(m // tm,),
        in_specs=[pl.BlockSpec((tm, D_MODEL), row),
                  pl.BlockSpec((TOP_K, tm, D_MODEL), lambda i: (0, i, 0)),
                  pl.BlockSpec((tm, LANES), row),
                  pl.BlockSpec((1, D_MODEL), const),
                  pl.BlockSpec((1, D_MODEL), const),
                  pl.BlockSpec((D_MODEL, D_MODEL), const),
                  pl.BlockSpec((1, tm, PLE_DIM), lambda i: (layer, jnp.minimum(i, n_p - 1), 0)),
                  pl.BlockSpec((1, tm, PLE_DIM), lambda i: (layer, jnp.clip(i - n_p, 0, n_s - 1), 0)),
                  pl.BlockSpec((PLE_DIM, D_MODEL), const)],
        out_specs=pl.BlockSpec((tm, D_MODEL), row),
        out_shape=jax.ShapeDtypeStruct((m, D_MODEL), F32),
        compiler_params=_cparams(("parallel",)),
        name="finish",
    )(x1, yk, tg, g.reshape(1, D_MODEL), b.reshape(1, D_MODEL), w_pg, p_p, p_s, w_pp)


def _moe_and_finish(layer, x1, te, tg, p_p, p_s, ln2_g, ln2_b, w_gu, b_gu, w_down, b_down, w_pg, w_pp):
    t = x1.shape[0]
    a = t * TOP_K
    rank, cnt = _route_ranks(te)
    counts = cnt[0, :N_EXPERTS]
    padded = (counts + MOE_ROWS - 1) // MOE_ROWS * MOE_ROWS
    pend = jnp.cumsum(padded)
    pstart = pend - padded
    e4 = te[:, :TOP_K]
    dest = pstart[e4] + rank[:, :TOP_K]
    nb = a // MOE_ROWS + N_EXPERTS
    block_start = jnp.arange(nb, dtype=I32) * MOE_ROWS
    block_e = jnp.minimum(jnp.sum((pend[None, :] <= block_start[:, None]).astype(I32), axis=1), N_EXPERTS - 1)
    n_used = (pend[-1:] // MOE_ROWS).astype(I32)
    flat = jnp.arange(a, dtype=I32)
    tok_sorted = (jnp.sort(e4.reshape(-1) * a + flat) % a) // TOP_K
    start = jnp.cumsum(counts) - counts
    slot_e = jnp.repeat(block_e, MOE_ROWS)
    entry = jnp.arange(nb * MOE_ROWS, dtype=I32) - pstart[slot_e] + start[slot_e]
    slot_tok = tok_sorted[jnp.clip(entry, 0, a - 1)]
    xb = x1[slot_tok]
    yb = _experts(xb, block_e, n_used, w_gu, b_gu, w_down, b_down, layer)
    yk = yb[dest.T]
    return _finish(x1, yk, tg, ln2_g, ln2_b, w_pg, p_p, p_s, layer, w_pp)


def kernel(x_prompt, x_sample, cache_k, cache_v, cache_logf, state_ssm, state_conv, page_table, p_prompt, p_sample,
           fox_w_in, fox_b_f, fox_w_out, ssd_w_in, ssd_conv_w, ssd_conv_b, ssd_dt_bias, ssd_a_log, ssd_d,
           ssd_norm_w, ssd_w_out, ln1_g, ln1_b, ln2_g, ln2_b, moe_w_router, moe_b_router, moe_w_gu, moe_b_gu,
           moe_w_down, moe_b_down, ple_w_gate, ple_w_proj):
    bsz, seq, _ = x_prompt.shape
    bd, ld, _ = x_sample.shape
    tp = bsz * seq
    ts = bd * ld
    x = jnp.concatenate([x_prompt.reshape(tp, D_MODEL), x_sample.reshape(ts, D_MODEL)], axis=0)
    p_p = p_prompt.reshape(DEPTH, tp, PLE_DIM)
    p_s = p_sample.reshape(DEPTH, ts, PLE_DIM)

    prompt, sample = (0, tp), (tp, ts)
    kp, vp, lfp, hp, cp = [], [], [], [], []
    ks_, vs_, lfs, hs, cs = [], [], [], [], []
    for i in range(DEPTH):
        j = i // N_MIXERS
        if i % N_MIXERS == 0:
            w_in = fox_w_in[j]
            wq = w_in[:, :FOX_WIDTH].astype(BF16)
            wk = w_in[:, FOX_WIDTH:2 * FOX_WIDTH].astype(BF16)
            wv = w_in[:, 2 * FOX_WIDTH:3 * FOX_WIDTH].astype(BF16)
            wf = w_in[:, 3 * FOX_WIDTH:]
            kt_p = _mm_t(x, wk.T, bsz, seq)
            vt_p = _mm_t(x, wv.T, bsz, seq)
            k_s3 = _mm(x, wk, F32, sample).reshape(bd, ld, FOX_WIDTH)
            v_s3 = _mm(x, wv, F32, sample).reshape(bd, ld, FOX_WIDTH)
            lf_s3 = _gate_rows(x, wf, fox_b_f[j], "log_sigmoid", sample).reshape(bd, ld, FOX_HEADS)
            q_p = _mm(x, wq, BF16, prompt, scale=FOX_SCALE * LOG2E).reshape(bsz, seq, FOX_WIDTH)
            q_s = _mm(x, wq, F32, sample, scale=FOX_SCALE * LOG2E).reshape(bd, ld, FOX_WIDTH)
            lft_p, c4 = _gate_cols(x, bsz, seq, wf.T, fox_b_f[j], "log_sigmoid", True, LOG2E)
            c4 = c4.reshape(bsz, FOX_HEADS, 1, seq)
            o_p = _fox_attn(q_p, kt_p, vt_p, c4)
            o_s = _fox_decode(q_s, k_s3, v_s3, lf_s3, cache_k, cache_v, cache_logf, page_table, j)
            o_p, o_s = o_p.reshape(tp, FOX_WIDTH), o_s.reshape(ts, FOX_WIDTH)
            w_out = fox_w_out[j].astype(BF16)
            kp.append(kt_p.reshape(bsz, FOX_HEADS, FOX_HEAD_DIM, seq).transpose(0, 3, 1, 2))
            vp.append(vt_p.reshape(bsz, FOX_HEADS, FOX_HEAD_DIM, seq).transpose(0, 3, 1, 2))
            lfp.append(lft_p.transpose(0, 2, 1))
            ks_.append(k_s3.reshape(bd, ld, FOX_HEADS, FOX_HEAD_DIM))
            vs_.append(v_s3.reshape(bd, ld, FOX_HEADS, FOX_HEAD_DIM))
            lfs.append(lf_s3)
        else:
            w_in = ssd_w_in[j]
            wz = w_in[:, :D_INNER].astype(BF16)
            wx = w_in[:, D_INNER:D_INNER + CONV_DIM].astype(BF16)
            wd = w_in[:, D_INNER + CONV_DIM:]
            z_p3 = _mm(x, wz, F32, prompt).reshape(bsz, seq, D_INNER)
            z_s3 = _mm(x, wz, F32, sample).reshape(bd, ld, D_INNER)
            xbc_p3 = _mm(x, wx, F32, prompt).reshape(bsz, seq, CONV_DIM)
            xbc_s3 = _mm(x, wx, F32, sample).reshape(bd, ld, CONV_DIM)
            dt_p3 = _gate_rows(x, wd, ssd_dt_bias[j], "softplus", prompt).reshape(bsz, seq, SSD_HEADS)
            dt_s3 = _gate_rows(x, wd, ssd_dt_bias[j], "softplus", sample).reshape(bd, ld, SSD_HEADS)
            dtt = _gate_cols(x, bsz, seq, wd.T, ssd_dt_bias[j], "softplus", False)
            d_lanes = jnp.repeat(ssd_d[j], SSD_HEAD_DIM).reshape(1, D_INNER)
            y_p, h_p = _ssd_prompt(xbc_p3, z_p3, dt_p3, dtt,
                                   ssd_conv_w[j], ssd_conv_b[j], ssd_a_log[j], d_lanes, ssd_norm_w[j])
            y_s, h_s = _ssd_sample(xbc_s3, state_conv[j], z_s3, dt_s3, state_ssm, j,
                                   ssd_conv_w[j], ssd_conv_b[j], ssd_a_log[j], d_lanes, ssd_norm_w[j])
            o_p, o_s = y_p.reshape(tp, D_INNER), y_s.reshape(ts, D_INNER)
            w_out = ssd_w_out[j].astype(BF16)
            tail = CONV_K - 1
            hp.append(h_p)
            cp.append(xbc_p3[:, seq - tail:, :])
            hs.append(h_s)
            cs.append(jnp.concatenate([state_conv[j], xbc_s3], axis=1)[:, ld:, :])
        x1, te, tg = _post_mix(o_p, o_s, w_out, x, ln1_g[i], ln1_b[i], moe_w_router[i], moe_b_router[i])
        x = _moe_and_finish(i, x1, te, tg, p_p, p_s, ln2_g[i], ln2_b[i], moe_w_gu, moe_b_gu,
                            moe_w_down, moe_b_down, ple_w_gate[i].astype(BF16), ple_w_proj[i].astype(BF16))
    y_prompt = x[:tp].reshape(bsz, seq, D_MODEL)
    y_sample = x[tp:].reshape(bd, ld, D_MODEL)
    return (y_prompt, y_sample, jnp.stack(kp), jnp.stack(vp), jnp.stack(lfp), jnp.stack(hp), jnp.stack(cp),
            jnp.stack(ks_), jnp.stack(vs_), jnp.stack(lfs), jnp.stack(hs), jnp.stack(cs))
```
